```python
import jax, jax.numpy as jnp
from jax import lax
import numpy as np


D_MODEL = 2048
BATCH = 4
SEQ = 2048
DEPTH = 2

MEM_LEN = 256
GRID_W = 64
HEAD_DIM = 128
RMS_EPS = 1e-6

NA_HEADS = 6
NA_MAX_KH = 8
NA_KW = 16
NA_W = NA_HEADS * HEAD_DIM

FN_GROUPS = 6
FN_GROUP_W = 128
FN_W = FN_GROUPS * FN_GROUP_W

DIL_PATTERNS = ((128, 1), (512, 4), (2048, 16))
DIL_GROUPS = 3
DIL_HEADS_PER_GROUP = 2
DIL_HEADS = DIL_GROUPS * DIL_HEADS_PER_GROUP
DIL_W = DIL_HEADS * HEAD_DIM
DIL_OUT_W = DIL_HEADS_PER_GROUP * HEAD_DIM
Q_BLOCK = 128
ROT_DIM = HEAD_DIM // 4
ROPE_THETA = 500000.0

MEM_HEADS = 4
MEM_HEAD_DIM = 192
MEM_W = MEM_HEADS * MEM_HEAD_DIM

IN_COLS = 3 * NA_W + 3 * DIL_W + FN_W + MEM_W
N_BRANCH = 4

D_FF = ((8 * D_MODEL // 3 + 255) // 256) * 256
N_EXPERTS = 8
TOP_K = 2
EXPERT_FF = 7 * D_MODEL // 2
N_DENSE = (DEPTH + 1) // 2
N_MOE = DEPTH // 2

kernel_name = "hybrid_natten_fnet_dilated_memory_moe_encoder"


def rms_norm(x, g):
    xf = x.astype(jnp.float32)
    y = xf * lax.rsqrt(jnp.mean(xf * xf, axis=-1, keepdims=True) + RMS_EPS)
    return (y * g.astype(jnp.float32)).astype(x.dtype)


def partial_rotary(x, pos):
    half = ROT_DIM // 2
    inv_freq = ROPE_THETA ** (-2.0 * jnp.arange(half, dtype=jnp.float32) / ROT_DIM)
    ang = pos.astype(jnp.float32)[:, None] * inv_freq[None, :]
    cos = jnp.cos(ang)[None, :, None, :]
    sin = jnp.sin(ang)[None, :, None, :]
    xf = x.astype(jnp.float32)
    x1 = xf[..., :half]
    x2 = xf[..., half:ROT_DIM]
    out = jnp.concatenate([x1 * cos - x2 * sin, x2 * cos + x1 * sin, xf[..., ROT_DIM:]], axis=-1)
    return out.astype(x.dtype)


def neighbourhood_attention(q, k, v, rpb):
    b, s, h, hd = q.shape
    rows = s // GRID_W
    kh = min(NA_MAX_KH, rows)
    row_start = np.clip(np.arange(rows) - kh // 2, 0, rows - kh)
    col_start = np.clip(np.arange(GRID_W) - NA_KW // 2, 0, GRID_W - NA_KW)
    col_idx = col_start[:, None] + np.arange(NA_KW)[None, :]
    d_row = row_start[:, None] + np.arange(kh)[None, :] - np.arange(rows)[:, None] + (NA_MAX_KH - 1)
    d_col = col_idx - np.arange(GRID_W)[:, None] + (NA_KW - 1)
    bias = rpb[:, d_row[:, :, None, None], d_col[None, None, :, :]]
    bias = bias.transpose(1, 0, 3, 2, 4).astype(jnp.float32)
    q_rows = (q * hd ** -0.5).reshape(b, rows, GRID_W, h, hd).transpose(1, 0, 2, 3, 4)
    k_grid = k.reshape(b, rows, GRID_W, h, hd)
    v_grid = v.reshape(b, rows, GRID_W, h, hd)

    def row_block(args):
        q_row, r0, b_row = args
        k_blk = lax.dynamic_slice_in_dim(k_grid, r0, kh, axis=1)[:, :, col_idx]
        v_blk = lax.dynamic_slice_in_dim(v_grid, r0, kh, axis=1)[:, :, col_idx]
        sc = jnp.einsum("bchd,bkcjhd->bhckj", q_row, k_blk).astype(jnp.float32) + b_row[None]
        p = jax.nn.softmax(sc.reshape(b, h, GRID_W, kh * NA_KW), axis=-1).reshape(sc.shape)
        return jnp.einsum("bhckj,bkcjhd->bchd", p.astype(v_blk.dtype), v_blk)

    out = lax.map(row_block, (q_rows, jnp.asarray(row_start, jnp.int32), bias))
    return out.transpose(1, 0, 2, 3, 4).reshape(b, s, h * hd)


def dilated_attention(q, k, v):
    b, s, _, hd = q.shape
    g, hg = DIL_GROUPS, DIL_HEADS_PER_GROUP
    offsets = jnp.asarray(np.stack([np.arange(-(w // (2 * d)), w // (2 * d) + 1) * d
                                    for (w, d) in DIL_PATTERNS]), jnp.int32)
    group_ix = np.arange(g)[:, None, None]
    n_blk = s // Q_BLOCK
    q_blocks = (q * hd ** -0.5).reshape(b, n_blk, Q_BLOCK, g, hg, hd).transpose(1, 0, 2, 3, 4, 5)
    k_g = k.reshape(b, s, g, hg, hd)
    v_g = v.reshape(b, s, g, hg, hd)

    def query_block(args):
        q_blk, start = args
        idx = start + jnp.arange(Q_BLOCK, dtype=jnp.int32)[None, :, None] + offsets[:, None, :]
        valid = (idx >= 0) & (idx < s)
        idx = jnp.clip(idx, 0, s - 1)
        k_sel = k_g[:, idx, group_ix]
        v_sel = v_g[:, idx, group_ix]
        sc = jnp.einsum("bqghd,bgqjhd->bghqj", q_blk, k_sel).astype(jnp.float32)
        sc = jnp.where(valid[None, :, None], sc, -jnp.inf)
        lse = jax.nn.logsumexp(sc, axis=-1)
        p = jnp.exp(sc - lse[..., None])
        o = jnp.einsum("bghqj,bgqjhd->bghqd", p.astype(v_sel.dtype), v_sel)
        w_grp = jax.nn.softmax(lse, axis=1)
        return jnp.einsum("bghq,bghqd->bqhd", w_grp.astype(o.dtype), o)

    starts = jnp.arange(n_blk, dtype=jnp.int32) * Q_BLOCK
    out = lax.map(query_block, (q_blocks, starts))
    return out.transpose(1, 0, 2, 3, 4).reshape(b, s, hg * hd)


def fourier_mix(u):
    b, s, _ = u.shape
    ug = u.astype(jnp.float32).reshape(b, s, FN_GROUPS, FN_GROUP_W)
    f = jnp.fft.fft2(ug, axes=(1, 3), norm="ortho")
    return jnp.real(f).reshape(b, s, FN_W).astype(u.dtype)


def memory_attention(q, k, v):
    sc = jnp.einsum("bshd,bmhd->bhsm", q * MEM_HEAD_DIM ** -0.5, k).astype(jnp.float32)
    p = jax.nn.softmax(sc, axis=-1)
    o = jnp.einsum("bhsm,bmhd->bshd", p.astype(v.dtype), v)
    return o.reshape(q.shape[0], q.shape[1], MEM_W)


def hybrid_mixer(h, mem_n, pos, w_in, rpb, w_mem_kv, w_br_a, w_br_b, w_br_c, w_br_m, w_gate, b_gate, w_out):
    b, s, d = h.shape
    z = h @ w_in
    splits = [int(c) for c in np.cumsum([NA_W] * 3 + [DIL_W] * 3 + [FN_W])]
    qa, ka, va, qc, kc, vc, ub, qm = jnp.split(z, splits, axis=-1)
    heads = lambda t, n, dh: t.reshape(t.shape[0], t.shape[1], n, dh)
    y_a = neighbourhood_attention(heads(qa, NA_HEADS, HEAD_DIM), heads(ka, NA_HEADS, HEAD_DIM),
                                  heads(va, NA_HEADS, HEAD_DIM), rpb)
    y_b = fourier_mix(ub)
    y_c = dilated_attention(partial_rotary(heads(qc, DIL_HEADS, HEAD_DIM), pos),
                            partial_rotary(heads(kc, DIL_HEADS, HEAD_DIM), pos),
                            heads(vc, DIL_HEADS, HEAD_DIM))
    kv = mem_n @ w_mem_kv
    km, vm = jnp.split(kv, 2, axis=-1)
    y_m = memory_attention(heads(qm, MEM_HEADS, MEM_HEAD_DIM), heads(km, MEM_HEADS, MEM_HEAD_DIM),
                           heads(vm, MEM_HEADS, MEM_HEAD_DIM))
    gates = jax.nn.sigmoid((h @ w_gate + b_gate).astype(jnp.float32)).astype(h.dtype).reshape(b, s, N_BRANCH, d)
    merged = (gates[:, :, 0] * (y_a @ w_br_a) + gates[:, :, 1] * (y_b @ w_br_b)
              + gates[:, :, 2] * (y_c @ w_br_c) + gates[:, :, 3] * (y_m @ w_br_m))
    return merged @ w_out


def swiglu(t, wg, wu, wd):
    return (jax.nn.silu(t @ wg) * (t @ wu)) @ wd


def moe_swiglu(h, router, wg, wu, wd):
    b, s, d = h.shape
    t = h.reshape(b * s, d)
    logits = (t @ router).astype(jnp.float32)
    top_v, top_i = lax.top_k(logits, TOP_K)
    gate = jax.nn.softmax(top_v, axis=-1)
    combine = jnp.sum(jax.nn.one_hot(top_i, N_EXPERTS, dtype=jnp.float32) * gate[..., None], axis=1)
    combine = combine.astype(t.dtype)
    out = jnp.zeros_like(t)
    for e in range(N_EXPERTS):
        out = out + combine[:, e:e + 1] * swiglu(t, wg[e], wu[e], wd[e])
    return out.reshape(b, s, d)


def setup_inputs(seed: int = 0) -> dict:
    key = jax.random.key(seed)
    ks = jax.random.split(key, 24)
    f32 = jnp.float32
    D = D_MODEL

    def nrm(k, shape, scale):
        return jax.random.normal(k, shape, f32) * scale

    def gain(k, shape):
        return 1.0 + 0.05 * jax.random.normal(k, shape, f32)

    return {
        "x": nrm(ks[0], (BATCH, SEQ, D), 1.0),
        "mem": nrm(ks[1], (BATCH, MEM_LEN, D), 1.0),
        "norm_mix": gain(ks[2], (DEPTH, D)),
        "w_in": nrm(ks[3], (DEPTH, D, IN_COLS), D ** -0.5),
        "rpb": nrm(ks[4], (DEPTH, NA_HEADS, 2 * NA_MAX_KH - 1, 2 * NA_KW - 1), 0.1),
        "norm_mem": gain(ks[5], (DEPTH, D)),
        "w_mem_kv": nrm(ks[6], (DEPTH, D, 2 * MEM_W), D ** -0.5),
        "w_br_a": nrm(ks[7], (DEPTH, NA_W, D), NA_W ** -0.5),
        "w_br_b": nrm(ks[8], (DEPTH, FN_W, D), FN_W ** -0.5),
        "w_br_c": nrm(ks[9], (DEPTH, DIL_OUT_W, D), DIL_OUT_W ** -0.5),
        "w_br_m": nrm(ks[10], (DEPTH, MEM_W, D), MEM_W ** -0.5),
        "w_gate": nrm(ks[11], (DEPTH, D, N_BRANCH * D), D ** -0.5),
        "b_gate": nrm(ks[12], (DEPTH, N_BRANCH * D), 0.1),
        "w_out": nrm(ks[13], (DEPTH, D, D), D ** -0.5),
        "norm_ffn": gain(ks[14], (DEPTH, D)),
        "dense_w_gate": nrm(ks[15], (N_DENSE, D, D_FF), D ** -0.5),
        "dense_w_up": nrm(ks[16], (N_DENSE, D, D_FF), D ** -0.5),
        "dense_w_down": nrm(ks[17], (N_DENSE, D_FF, D), D_FF ** -0.5),
        "router": nrm(ks[18], (N_MOE, D, N_EXPERTS), D ** -0.5),
        "exp_w_gate": nrm(ks[19], (N_MOE, N_EXPERTS, D, EXPERT_FF), D ** -0.5),
        "exp_w_up": nrm(ks[20], (N_MOE, N_EXPERTS, D, EXPERT_FF), D ** -0.5),
        "exp_w_down": nrm(ks[21], (N_MOE, N_EXPERTS, EXPERT_FF, D), EXPERT_FF ** -0.5),
        "final_norm": gain(ks[22], (D,)),
    }


def reference(x, mem, norm_mix, w_in, rpb, norm_mem, w_mem_kv, w_br_a, w_br_b, w_br_c, w_br_m,
              w_gate, b_gate, w_out, norm_ffn, dense_w_gate, dense_w_up, dense_w_down,
              router, exp_w_gate, exp_w_up, exp_w_down, final_norm):
    pos = jnp.arange(x.shape[1], dtype=jnp.int32)
    for layer in range(DEPTH):
        h = rms_norm(x, norm_mix[layer])
        mem_n = rms_norm(mem, norm_mem[layer])
        x = x + hybrid_mixer(h, mem_n, pos, w_in[layer], rpb[layer], w_mem_kv[layer],
                             w_br_a[layer], w_br_b[layer], w_br_c[layer], w_br_m[layer],
                             w_gate[layer], b_gate[layer], w_out[layer])
        h = rms_norm(x, norm_ffn[layer])
        i = layer // 2
        if layer % 2 == 0:
            x = x + swiglu(h, dense_w_gate[i], dense_w_up[i], dense_w_down[i])
        else:
            x = x + moe_swiglu(h, router[i], exp_w_gate[i], exp_w_up[i], exp_w_down[i])
    return rms_norm(x, final_norm)
```

```python
import functools

import numpy as np
import jax
import jax.numpy as jnp
from jax import lax
from jax.experimental import pallas as pl
from jax.experimental.pallas import tpu as pltpu

F32 = jnp.float32
BF16 = jnp.bfloat16

D_MODEL = 2048
GRID_W = 64
HEAD_DIM = 128
RMS_EPS = 1e-6
NA_HEADS = 6
NA_KH = 8
NA_KW = 16
NA_W = NA_HEADS * HEAD_DIM
FN_GROUPS = 6
FN_GROUP_W = 128
FN_W = FN_GROUPS * FN_GROUP_W
DIL_DILATIONS = (1, 4, 16)
DIL_HALF_WINDOW = 64
DIL_HEADS_PER_GROUP = 2
DIL_GROUP_W = DIL_HEADS_PER_GROUP * HEAD_DIM
Q_BLOCK = 128
ROT_DIM = HEAD_DIM // 4
ROPE_THETA = 500000.0
MEM_HEADS = 4
MEM_HEAD_DIM = 192
MEM_HEAD_PAD = 256
MEM_W = MEM_HEADS * MEM_HEAD_DIM
MEM_W_PAD = MEM_HEADS * MEM_HEAD_PAD
N_EXPERTS = 8
TOP_K = 2
MASK_VALUE = -1e30

V7X_VMEM_BYTES = 64 * 1024 * 1024
VMEM_LIMIT_BYTES = V7X_VMEM_BYTES - 4 * 1024 * 1024
LANES = 128


def _params(n_axes):
    return pltpu.CompilerParams(dimension_semantics=("arbitrary",) * n_axes,
                                vmem_limit_bytes=VMEM_LIMIT_BYTES)


def _dot(a, b):
    return jnp.dot(a, b, preferred_element_type=F32)


def _dot_nt(a, b):
    return lax.dot_general(a, b, (((1,), (1,)), ((), ())), preferred_element_type=F32)


def _rmsnorm_body(x, g):
    return x * lax.rsqrt(jnp.mean(x * x, axis=-1, keepdims=True) + RMS_EPS) * g


def _rmsnorm_kernel(x_ref, g_ref, o_ref):
    o_ref[...] = _rmsnorm_body(x_ref[...], g_ref[...]).astype(o_ref.dtype)


def rmsnorm(x, g, out_dtype, tm=512):
    m, d = x.shape
    return pl.pallas_call(
        _rmsnorm_kernel,
        grid=(m // tm,),
        in_specs=[pl.BlockSpec((tm, d), lambda i: (i, 0)),
                  pl.BlockSpec((1, d), lambda i: (0, 0))],
        out_specs=pl.BlockSpec((tm, d), lambda i: (i, 0)),
        out_shape=jax.ShapeDtypeStruct((m, d), out_dtype),
        compiler_params=_params(1),
        name="rmsnorm",
    )(x, g.reshape(1, d))


def _mm_kernel(a_ref, w_ref, o_ref):
    o_ref[...] = _dot(a_ref[...], w_ref[...].astype(BF16)).astype(o_ref.dtype)


def _mm_res_kernel(a_ref, w_ref, r_ref, o_ref):
    o_ref[...] = (r_ref[...] + _dot(a_ref[...], w_ref[...].astype(BF16))).astype(o_ref.dtype)


def matmul(a, w, layer, *, tm, tn, n_cols=None, out_dtype=BF16, residual=None, name="matmul"):
    m, k = a.shape
    n = w.shape[2] if n_cols is None else n_cols
    in_specs = [pl.BlockSpec((tm, k), lambda i, j: (i, 0)),
                pl.BlockSpec((None, k, tn), lambda i, j: (layer, 0, j))]
    args = [a, w]
    kernel = _mm_kernel
    if residual is not None:
        in_specs.append(pl.BlockSpec((tm, tn), lambda i, j: (i, j)))
        args.append(residual)
        kernel = _mm_res_kernel
    return pl.pallas_call(
        kernel,
        grid=(m // tm, n // tn),
        in_specs=in_specs,
        out_specs=pl.BlockSpec((tm, tn), lambda i, j: (i, j)),
        out_shape=jax.ShapeDtypeStruct((m, n), out_dtype),
        compiler_params=_params(2),
        name=name,
    )(*args)


NA_Q_ROWS = 4
NA_KEY_ROWS = NA_KH + NA_Q_ROWS


def _na_key_row_start(step, rows):
    return np.clip(step * NA_Q_ROWS - NA_KH // 2, 0, rows - NA_KEY_ROWS)


def _na_bias_table(rpb, rows):
    col = np.arange(GRID_W)
    col_start = np.clip(col - NA_KW // 2, 0, GRID_W - NA_KW)
    rel_col = col[None, :] - col[:, None] + (NA_KW - 1)
    col_ok = (col[None, :] >= col_start[:, None]) & (col[None, :] < col_start[:, None] + NA_KW)
    slab = rpb[:, :, np.clip(rel_col, 0, 2 * NA_KW - 2)].astype(F32)
    n_steps = rows // NA_Q_ROWS
    rel_row = np.zeros((n_steps, NA_Q_ROWS, NA_KEY_ROWS), np.int64)
    row_ok = np.zeros((n_steps, NA_Q_ROWS, NA_KEY_ROWS), bool)
    for step in range(n_steps):
        for i in range(NA_Q_ROWS):
            r = step * NA_Q_ROWS + i
            win = np.clip(r - NA_KH // 2, 0, rows - NA_KH)
            key_row = _na_key_row_start(step, rows) + np.arange(NA_KEY_ROWS)
            row_ok[step, i] = (key_row >= win) & (key_row < win + NA_KH)
            rel_row[step, i] = key_row - r + (NA_KH - 1)
    assert all((rel_row[s_] * row_ok[s_] == rel_row[1] * row_ok[1]).all() and (row_ok[s_] == row_ok[1]).all()
               for s_ in range(1, n_steps - 1))
    cases = [0, 1, n_steps - 1]
    rel = np.clip(rel_row[cases], 0, 2 * NA_KH - 2)
    ok = row_ok[cases][:, None, :, None, :, None] & col_ok[None, None, None, :, None, :]
    bias = slab[:, rel]
    bias = bias.transpose(1, 0, 2, 4, 3, 5)
    bias = jnp.where(ok, bias, MASK_VALUE)
    return bias.reshape(3, NA_HEADS, NA_Q_ROWS * GRID_W, NA_KEY_ROWS * GRID_W)


def _na_kernel(q_ref, k_ref, v_ref, b_ref, o_ref, *, rows):
    step = pl.program_id(1)
    key_row0 = jnp.clip(step * NA_Q_ROWS - NA_KH // 2, 0, rows - NA_KEY_ROWS)
    k0 = pl.multiple_of(key_row0 * GRID_W, GRID_W)
    n_keys = NA_KEY_ROWS * GRID_W
    scale = HEAD_DIM ** -0.5
    for h in range(NA_HEADS):
        cols = slice(h * HEAD_DIM, (h + 1) * HEAD_DIM)
        k = k_ref[0, pl.ds(k0, n_keys), cols]
        v = v_ref[0, pl.ds(k0, n_keys), cols]
        s = _dot_nt(q_ref[0, :, cols], k) * scale + b_ref[0, h]
        p = jnp.exp(s - jnp.max(s, axis=-1, keepdims=True))
        den = jnp.sum(p, axis=-1, keepdims=True)
        o_ref[0, :, cols] = (_dot(p.astype(BF16), v) / den).astype(o_ref.dtype)


def neighbourhood_attention(z, rpb):
    b, s, _ = z.shape
    rows = s // GRID_W
    n_steps = rows // NA_Q_ROWS
    bias = _na_bias_table(rpb, rows)
    tq = NA_Q_ROWS * GRID_W
    bias_case = lambda i, r: (jnp.where(r == 0, 0, jnp.where(r == n_steps - 1, 2, 1)), 0, 0, 0)
    return pl.pallas_call(
        functools.partial(_na_kernel, rows=rows),
        grid=(b, n_steps),
        in_specs=[pl.BlockSpec((1, tq, NA_W), lambda i, r: (i, r, 0)),
                  pl.BlockSpec((1, s, NA_W), lambda i, r: (i, 0, 1)),
                  pl.BlockSpec((1, s, NA_W), lambda i, r: (i, 0, 2)),
                  pl.BlockSpec((1,) + bias.shape[1:], bias_case)],
        out_specs=pl.BlockSpec((1, tq, NA_W), lambda i, r: (i, r, 0)),
        out_shape=jax.ShapeDtypeStruct((b, s, NA_W), BF16),
        compiler_params=_params(2),
        name="neighbourhood_attention",
    )(z, z, z, bias)


def _dft_tables(s):
    n = np.arange(s, dtype=np.int64)
    ang = 2.0 * np.pi * ((n[:, None] * n[None, :]) % s) / s
    pos = np.concatenate([np.cos(ang), -np.sin(ang)], axis=1)
    c = np.arange(FN_GROUP_W, dtype=np.int64)
    ang_c = 2.0 * np.pi * ((c[:, None] * c[None, :]) % FN_GROUP_W) / FN_GROUP_W
    chan = np.concatenate([np.cos(ang_c), np.sin(ang_c)], axis=1)
    return jnp.asarray(pos, F32).astype(BF16), jnp.asarray(chan, F32).astype(BF16)


def _fourier_kernel(u_ref, chan_ref, pos_ref, o_ref, v_scr, *, s, norm):
    @pl.when(pl.program_id(1) == 0)
    def _():
        for g in range(FN_GROUPS):
            cols = slice(g * FN_GROUP_W, (g + 1) * FN_GROUP_W)
            t = _dot(u_ref[0, :, cols], chan_ref[...])
            v_scr[0:s, cols] = t[:, :FN_GROUP_W].astype(BF16)
            v_scr[s:2 * s, cols] = t[:, FN_GROUP_W:].astype(BF16)

    o_ref[0] = (_dot(pos_ref[...], v_scr[...]) * norm).astype(o_ref.dtype)


def fourier_mix(z, ts=512):
    b, s, _ = z.shape
    pos_tab, chan_tab = _dft_tables(s)
    norm = float((s * FN_GROUP_W) ** -0.5)
    return pl.pallas_call(
        functools.partial(_fourier_kernel, s=s, norm=norm),
        grid=(b, s // ts),
        in_specs=[pl.BlockSpec((1, s, FN_W), lambda i, j: (i, 0, 6)),
                  pl.BlockSpec(chan_tab.shape, lambda i, j: (0, 0)),
                  pl.BlockSpec((ts, 2 * s), lambda i, j: (j, 0))],
        out_specs=pl.BlockSpec((1, ts, FN_W), lambda i, j: (i, j, 0)),
        out_shape=jax.ShapeDtypeStruct((b, s, FN_W), BF16),
        scratch_shapes=[pltpu.VMEM((2 * s, FN_W), BF16)],
        compiler_params=_params(2),
        name="fourier_mix",
    )(z, chan_tab, pos_tab)


def _rope_tables(s):
    half = ROT_DIM // 2
    inv_freq = ROPE_THETA ** (-2.0 * jnp.arange(half, dtype=F32) / ROT_DIM)
    ang = jnp.arange(s, dtype=jnp.int32).astype(F32)[:, None] * inv_freq[None, :]
    cos, sin = jnp.cos(ang), jnp.sin(ang)
    ones = jnp.ones((s, HEAD_DIM - ROT_DIM), F32)
    zeros_half = jnp.zeros((s, half), F32)
    zeros_rest = jnp.zeros((s, HEAD_DIM - ROT_DIM), F32)
    cos_t = jnp.concatenate([cos, cos, ones], axis=1)
    sin_lo = jnp.concatenate([-sin, zeros_half, zeros_rest], axis=1)
    sin_hi = jnp.concatenate([zeros_half, sin, zeros_rest], axis=1)
    return cos_t, sin_lo, sin_hi


def _band_attn_kernel(qkv_ref, cos_ref, slo_ref, shi_ref, o_ref, lse_ref, *, seq, n_keys):
    i = pl.program_id(1)
    half = ROT_DIM // 2
    q0 = pl.multiple_of(i * Q_BLOCK, Q_BLOCK)
    k0 = pl.multiple_of(jnp.clip(i * Q_BLOCK - DIL_HALF_WINDOW, 0, seq - n_keys), DIL_HALF_WINDOW)
    scale = HEAD_DIM ** -0.5

    def rope(x, rows, n):
        c = cos_ref[0, pl.ds(rows, n), :]
        lo = slo_ref[0, pl.ds(rows, n), :]
        hi = shi_ref[0, pl.ds(rows, n), :]
        return x * c + pltpu.roll(x, HEAD_DIM - half, 1) * lo + pltpu.roll(x, half, 1) * hi

    q_pos = q0 + lax.broadcasted_iota(jnp.int32, (Q_BLOCK, n_keys), 0)
    k_pos = k0 + lax.broadcasted_iota(jnp.int32, (Q_BLOCK, n_keys), 1)
    in_window = jnp.abs(k_pos - q_pos) <= DIL_HALF_WINDOW
    for h in range(DIL_HEADS_PER_GROUP):
        cols = slice(h * HEAD_DIM, (h + 1) * HEAD_DIM)
        kcols = slice(DIL_GROUP_W + h * HEAD_DIM, DIL_GROUP_W + (h + 1) * HEAD_DIM)
        vcols = slice(2 * DIL_GROUP_W + h * HEAD_DIM, 2 * DIL_GROUP_W + (h + 1) * HEAD_DIM)
        q = rope(qkv_ref[0, pl.ds(q0, Q_BLOCK), cols].astype(F32), q0, Q_BLOCK)
        k = rope(qkv_ref[0, pl.ds(k0, n_keys), kcols].astype(F32), k0, n_keys)
        v = qkv_ref[0, pl.ds(k0, n_keys), vcols]
        s = _dot_nt(q.astype(BF16), k.astype(BF16)) * scale
        s = jnp.where(in_window, s, MASK_VALUE)
        m = jnp.max(s, axis=-1, keepdims=True)
        p = jnp.exp(s - m)
        den = jnp.sum(p, axis=-1, keepdims=True)
        o_ref[0, :, cols] = _dot(p.astype(BF16), v) / den
        lse_ref[0, :, cols] = jnp.broadcast_to(m + jnp.log(den), (Q_BLOCK, HEAD_DIM))


def _band_attention(qkv, tables):
    n_seq, seq, _ = qkv.shape
    n_keys = min(seq, Q_BLOCK + 2 * DIL_HALF_WINDOW)
    table_spec = pl.BlockSpec((1, seq, HEAD_DIM), lambda n, i: (n % tables[0].shape[0], 0, 0))
    out_spec = pl.BlockSpec((1, Q_BLOCK, DIL_GROUP_W), lambda n, i: (n, i, 0))
    out_shape = jax.ShapeDtypeStruct((n_seq, seq, DIL_GROUP_W), F32)
    return pl.pallas_call(
        functools.partial(_band_attn_kernel, seq=seq, n_keys=n_keys),
        grid=(n_seq, seq // Q_BLOCK),
        in_specs=[pl.BlockSpec((1, seq, 3 * DIL_GROUP_W), lambda n, i: (n, 0, 0)),
                  table_spec, table_spec, table_spec],
        out_specs=[out_spec, out_spec],
        out_shape=[out_shape, out_shape],
        compiler_params=_params(2),
        name="dilated_band_attention",
    )(qkv, *tables)


def _group_combine_kernel(o0, o1, o2, l0, l1, l2, y_ref):
    lses = [l0[...], l1[...], l2[...]]
    top = jnp.maximum(jnp.maximum(lses[0], lses[1]), lses[2])
    ws = [jnp.exp(l - top) for l in lses]
    num = ws[0] * o0[...] + ws[1] * o1[...] + ws[2] * o2[...]
    y_ref[...] = (num / (ws[0] + ws[1] + ws[2])).astype(y_ref.dtype)


def dilated_attention(z):
    b, s, _ = z.shape
    tables = _rope_tables(s)
    outs, lses = [], []
    for g, d in enumerate(DIL_DILATIONS):
        parts = [z[:, :, (3 + t) * NA_W + g * DIL_GROUP_W:(3 + t) * NA_W + (g + 1) * DIL_GROUP_W]
                 for t in range(3)]
        qkv = jnp.concatenate(parts, axis=-1)
        qkv = qkv.reshape(b, s // d, d, 3 * DIL_GROUP_W).transpose(0, 2, 1, 3)
        qkv = qkv.reshape(b * d, s // d, 3 * DIL_GROUP_W)
        tabs = [t.reshape(s // d, d, HEAD_DIM).transpose(1, 0, 2) for t in tables]
        o, lse = _band_attention(qkv, tabs)
        unperm = lambda a: a.reshape(b, d, s // d, DIL_GROUP_W).transpose(0, 2, 1, 3).reshape(b * s, DIL_GROUP_W)
        outs.append(unperm(o))
        lses.append(unperm(lse))
    tm = 1024
    spec = pl.BlockSpec((tm, DIL_GROUP_W), lambda i: (i, 0))
    return pl.pallas_call(
        _group_combine_kernel,
        grid=(b * s // tm,),
        in_specs=[spec] * 6,
        out_specs=spec,
        out_shape=jax.ShapeDtypeStruct((b * s, DIL_GROUP_W), BF16),
        compiler_params=_params(1),
        name="dilated_group_combine",
    )(*outs, *lses)


def _mem_attn_kernel(q_ref, k_ref, v_ref, o_ref):
    scale = MEM_HEAD_DIM ** -0.5
    for h in range(MEM_HEADS):
        cols = slice(h * MEM_HEAD_PAD, (h + 1) * MEM_HEAD_PAD)
        s = _dot_nt(q_ref[0, :, cols], k_ref[0, :, cols]) * scale
        p = jnp.exp(s - jnp.max(s, axis=-1, keepdims=True))
        den = jnp.sum(p, axis=-1, keepdims=True)
        o_ref[0, :, cols] = (_dot(p.astype(BF16), v_ref[0, :, cols]) / den).astype(o_ref.dtype)


def memory_attention(q, kv, tq=512):
    b, s, _ = q.shape
    m = kv.shape[1]
    return pl.pallas_call(
        _mem_attn_kernel,
        grid=(b, s // tq),
        in_specs=[pl.BlockSpec((1, tq, MEM_W_PAD), lambda i, j: (i, j, 0)),
                  pl.BlockSpec((1, m, MEM_W_PAD), lambda i, j: (i, 0, 0)),
                  pl.BlockSpec((1, m, MEM_W_PAD), lambda i, j: (i, 0, 1))],
        out_specs=pl.BlockSpec((1, tq, MEM_W_PAD), lambda i, j: (i, j, 0)),
        out_shape=jax.ShapeDtypeStruct((b, s, MEM_W_PAD), BF16),
        compiler_params=_params(2),
        name="memory_attention",
    )(q, kv, kv)


def _pad_heads(w, axis):
    shape = list(w.shape)
    n = shape[axis] // MEM_HEAD_DIM
    w = w.reshape(shape[:axis] + [n, MEM_HEAD_DIM] + shape[axis + 1:])
    pad = [(0, 0)] * w.ndim
    pad[axis + 1] = (0, MEM_HEAD_PAD - MEM_HEAD_DIM)
    w = jnp.pad(w, pad)
    return w.reshape(shape[:axis] + [n * MEM_HEAD_PAD] + shape[axis + 1:])


def _merge_kernel(h_ref, ya_ref, yb_ref, yc_ref, ym_ref, wg0, wg1, wg2, wg3, bg0, bg1, bg2, bg3,
                  wa_ref, wb_ref, wc_ref, wm_ref, o_ref):
    h = h_ref[...]
    acc = None
    for y_ref, wg, bg, wbr in ((ya_ref, wg0, bg0, wa_ref), (yb_ref, wg1, bg1, wb_ref),
                               (yc_ref, wg2, bg2, wc_ref), (ym_ref, wg3, bg3, wm_ref)):
        gate = jax.nn.sigmoid(_dot(h, wg[...].astype(BF16)) + bg[...])
        term = gate * _dot(y_ref[...], wbr[...].astype(BF16))
        acc = term if acc is None else acc + term
    o_ref[...] = acc.astype(o_ref.dtype)


def _gate_block(i, j, *, layer, branch, n_tiles):
    return (layer, 0, branch * n_tiles + j)


def gated_merge(h, ys, w_gate, b_gate, w_brs, layers, tm=1024, tn=256):
    m, d = h.shape
    n_tiles = d // tn
    row = lambda width: pl.BlockSpec((tm, width), lambda i, j: (i, 0))
    gate_maps = [functools.partial(_gate_block, layer=layers[0], branch=br, n_tiles=n_tiles) for br in range(4)]
    gate_w = [pl.BlockSpec((None, d, tn), gm) for gm in gate_maps]
    gate_b = [pl.BlockSpec((None, 1, tn), gm) for gm in gate_maps]
    br_w = [pl.BlockSpec((None, w.shape[1], tn), functools.partial(lambda i, j, l: (l, 0, j), l=l))
            for w, l in zip(w_brs, layers[1:])]
    b_gate3 = b_gate.reshape(b_gate.shape[0], 1, -1)
    return pl.pallas_call(
        _merge_kernel,
        grid=(m // tm, n_tiles),
        in_specs=[row(d)] + [row(y.shape[1]) for y in ys] + gate_w + gate_b + br_w,
        out_specs=pl.BlockSpec((tm, tn), lambda i, j: (i, j)),
        out_shape=jax.ShapeDtypeStruct((m, d), BF16),
        compiler_params=_params(2),
        name="gated_merge",
    )(h, *ys, w_gate, w_gate, w_gate, w_gate, b_gate3, b_gate3, b_gate3, b_gate3, *w_brs)


def _silu_mul(g, u):
    return g * jax.nn.sigmoid(g) * u


def _swiglu_up_kernel(a_ref, wg_ref, wu_ref, o_ref):
    a = a_ref[...]
    o_ref[...] = _silu_mul(_dot(a, wg_ref[...].astype(BF16)), _dot(a, wu_ref[...].astype(BF16))).astype(o_ref.dtype)


def swiglu_up(a, wg, wu, layer, tm=1024, tf=512):
    m, k = a.shape
    f = wg.shape[2]
    return pl.pallas_call(
        _swiglu_up_kernel,
        grid=(m // tm, f // tf),
        in_specs=[pl.BlockSpec((tm, k), lambda i, j: (i, 0)),
                  pl.BlockSpec((None, k, tf), lambda i, j: (layer, 0, j)),
                  pl.BlockSpec((None, k, tf), lambda i, j: (layer, 0, j))],
        out_specs=pl.BlockSpec((tm, tf), lambda i, j: (i, j)),
        out_shape=jax.ShapeDtypeStruct((m, f), BF16),
        compiler_params=_params(2),
        name="swiglu_up",
    )(a, wg, wu)


def _router_kernel(x_ref, g_ref, r_ref, idx_ref, rank_ref, gate_ref, cnt_ref):
    @pl.when(pl.program_id(0) == 0)
    def _():
        cnt_ref[...] = jnp.zeros_like(cnt_ref)

    h = _rmsnorm_body(x_ref[...], g_ref[...])
    logits = jnp.dot(h, r_ref[...], preferred_element_type=F32, precision=lax.Precision.HIGHEST)
    lane = lax.broadcasted_iota(jnp.int32, logits.shape, 1).astype(F32)
    neg = -jnp.inf
    lg = jnp.where(lane < N_EXPERTS, logits, neg)
    v1 = jnp.max(lg, axis=-1, keepdims=True)
    i1 = jnp.min(jnp.where(lg == v1, lane, float(LANES)), axis=-1, keepdims=True)
    lg2 = jnp.where(lane == i1, neg, lg)
    v2 = jnp.max(lg2, axis=-1, keepdims=True)
    i2 = jnp.min(jnp.where(lg2 == v2, lane, float(LANES)), axis=-1, keepdims=True)
    e = jnp.exp(v2 - v1)
    g1 = 1.0 / (1.0 + e)
    g2 = e / (1.0 + e)
    onehot = jnp.where((lane == i1) | (lane == i2), 1.0, 0.0)
    tt = onehot.shape[0]
    earlier = lax.broadcasted_iota(jnp.int32, (tt, tt), 1) < lax.broadcasted_iota(jnp.int32, (tt, tt), 0)
    pos = _dot(jnp.where(earlier, 1.0, 0.0).astype(BF16), onehot.astype(BF16)) + cnt_ref[...]
    r1 = jnp.sum(jnp.where(lane == i1, pos, 0.0), axis=-1, keepdims=True)
    r2 = jnp.sum(jnp.where(lane == i2, pos, 0.0), axis=-1, keepdims=True)
    cnt_ref[...] += jnp.sum(onehot, axis=0, keepdims=True)
    idx_ref[...] = jnp.where(lane == 0, i1, jnp.where(lane == 1, i2, 0.0)).astype(jnp.int32)
    rank_ref[...] = jnp.where(lane == 0, r1, jnp.where(lane == 1, r2, 0.0)).astype(jnp.int32)
    gate_ref[...] = jnp.where(lane == 0, g1, jnp.where(lane == 1, g2, 0.0))


def route(x, g, router, tm=512):
    m, d = x.shape
    r_pad = jnp.pad(router, ((0, 0), (0, LANES - N_EXPERTS)))
    lane_spec = pl.BlockSpec((tm, LANES), lambda i: (i, 0))
    return pl.pallas_call(
        _router_kernel,
        grid=(m // tm,),
        in_specs=[pl.BlockSpec((tm, d), lambda i: (i, 0)),
                  pl.BlockSpec((1, d), lambda i: (0, 0)),
                  pl.BlockSpec((d, LANES), lambda i: (0, 0))],
        out_specs=[lane_spec, lane_spec, lane_spec, pl.BlockSpec((1, LANES), lambda i: (0, 0))],
        out_shape=[jax.ShapeDtypeStruct((m, LANES), jnp.int32),
                   jax.ShapeDtypeStruct((m, LANES), jnp.int32),
                   jax.ShapeDtypeStruct((m, LANES), F32),
                   jax.ShapeDtypeStruct((1, LANES), F32)],
        compiler_params=_params(1),
        name="route",
    )(x, g.reshape(1, d), r_pad)


def _dispatch_kernel(dest_ref, lo_ref, hi_ref, na_ref, x_hbm, g_ref, a_ref, src_ref, rows_ref, sem):
    j = pl.program_id(0)
    tm = rows_ref.shape[0]

    @pl.when(j == 0)
    def _():
        def clear(i, c):
            src_ref[i] = 0
            return c

        for e in range(N_EXPERTS):
            lax.fori_loop(lo_ref[e], hi_ref[e], clear, 0)

        def fill(i, c):
            src_ref[dest_ref[i]] = lax.shift_right_logical(i, 1)
            return c

        lax.fori_loop(0, dest_ref.shape[0], fill, 0, unroll=8)

    @pl.when(j < na_ref[0])
    def _():
        base = j * tm

        def row_copy(r):
            return pltpu.make_async_copy(x_hbm.at[pl.ds(src_ref[base + r], 1)], rows_ref.at[pl.ds(r, 1)], sem)

        def start(r, c):
            row_copy(r).start()
            return c

        def wait(r, c):
            row_copy(r).wait()
            return c

        lax.fori_loop(0, tm, start, 0, unroll=8)
        lax.fori_loop(0, tm, wait, 0, unroll=8)
        a_ref[...] = _rmsnorm_body(rows_ref[...], g_ref[...]).astype(a_ref.dtype)

    @pl.when(j >= na_ref[0])
    def _():
        a_ref[...] = jnp.zeros_like(a_ref)


def _combine_kernel(dest_ref, x_ref, gate_ref, g_ref, y_hbm, o_ref, rows_ref, sem, *, apply_norm):
    tt = x_ref.shape[0]
    base = pl.program_id(0) * tt

    def row_copy(r, k):
        row = dest_ref[TOP_K * (base + r) + k]
        return pltpu.make_async_copy(y_hbm.at[pl.ds(row, 1)], rows_ref.at[k, pl.ds(r, 1)], sem)

    def start(r, c):
        for k in range(TOP_K):
            row_copy(r, k).start()
        return c

    def wait(r, c):
        for k in range(TOP_K):
            row_copy(r, k).wait()
        return c

    lax.fori_loop(0, tt, start, 0, unroll=4)
    lax.fori_loop(0, tt, wait, 0, unroll=4)
    gates = gate_ref[...]
    y = x_ref[...]
    for k in range(TOP_K):
        y = y + gates[:, k:k + 1] * rows_ref[k]
    o_ref[...] = _rmsnorm_body(y, g_ref[...]) if apply_norm else y


def _moe_up_kernel(te_ref, na_ref, a_ref, wg_ref, wu_ref, o_ref):
    @pl.when(pl.program_id(1) < na_ref[0])
    def _():
        a = a_ref[...]
        o_ref[...] = _silu_mul(_dot(a, wg_ref[...].astype(BF16)), _dot(a, wu_ref[...].astype(BF16))).astype(o_ref.dtype)

    @pl.when(pl.program_id(1) >= na_ref[0])
    def _():
        o_ref[...] = jnp.zeros_like(o_ref)


def _moe_down_kernel(te_ref, na_ref, a_ref, w_ref, o_ref, *, tk):
    @pl.when(pl.program_id(1) < na_ref[0])
    def _():
        acc = None
        for c in range(a_ref.shape[1] // tk):
            part = _dot(a_ref[:, c * tk:(c + 1) * tk], w_ref[c * tk:(c + 1) * tk, :].astype(BF16))
            acc = part if acc is None else acc + part
        o_ref[...] = acc

    @pl.when(pl.program_id(1) >= na_ref[0])
    def _():
        o_ref[...] = jnp.zeros_like(o_ref)


def moe_layer(x, norm_g, router, wg, wu, wd, layer, out_g, tm=512, tf=512, tn=512, tk=1024, tt=256):
    assert TOP_K == 2
    t, d = x.shape
    _, n_exp, _, f = wg.shape
    n_rows = t * TOP_K
    n_tiles = n_rows // tm + n_exp
    idx, rank, gate, cnt = route(x, norm_g, router)
    counts = cnt[0, :n_exp].astype(jnp.int32)
    tiles_per = (counts + tm - 1) // tm
    tile_end = jnp.cumsum(tiles_per)
    start = (tile_end - tiles_per) * tm
    n_active = tile_end[-1:]
    tile_ids = jnp.arange(n_tiles, dtype=jnp.int32)
    tile_expert = jnp.sum((tile_ids[:, None] >= tile_end[None, :]).astype(jnp.int32), axis=1)
    last_expert = jnp.sum((n_active - 1 >= tile_end).astype(jnp.int32))
    tile_expert = jnp.where(tile_ids < n_active, tile_expert, last_expert).astype(jnp.int32)
    picked = idx[:, :TOP_K, None] == jnp.arange(n_exp, dtype=jnp.int32)
    dest = (jnp.sum(jnp.where(picked, start, 0), axis=-1) + rank[:, :TOP_K]).reshape(n_rows)

    row_tile = lambda j, na: jnp.minimum(j, na[0] - 1)
    a = pl.pallas_call(
        _dispatch_kernel,
        grid_spec=pltpu.PrefetchScalarGridSpec(
            num_scalar_prefetch=4,
            grid=(n_tiles,),
            in_specs=[pl.BlockSpec(memory_space=pl.ANY),
                      pl.BlockSpec((1, d), lambda j, dst, lo, hi, na: (0, 0))],
            out_specs=pl.BlockSpec((tm, d), lambda j, dst, lo, hi, na: (j, 0)),
            scratch_shapes=[pltpu.SMEM((n_tiles * tm,), jnp.int32),
                            pltpu.VMEM((tm, d), F32),
                            pltpu.SemaphoreType.DMA(())]),
        out_shape=jax.ShapeDtypeStruct((n_tiles * tm, d), BF16),
        compiler_params=_params(1),
        name="moe_dispatch",
    )(dest, start + counts, tile_end * tm, n_active, x, norm_g.reshape(1, d))
    up = pl.pallas_call(
        _moe_up_kernel,
        grid_spec=pltpu.PrefetchScalarGridSpec(
            num_scalar_prefetch=2,
            grid=(f // tf, n_tiles),
            in_specs=[pl.BlockSpec((tm, d), lambda c, j, te, na: (row_tile(j, na), 0)),
                      pl.BlockSpec((None, None, d, tf), lambda c, j, te, na: (layer, te[j], 0, c)),
                      pl.BlockSpec((None, None, d, tf), lambda c, j, te, na: (layer, te[j], 0, c))],
            out_specs=pl.BlockSpec((tm, tf), lambda c, j, te, na: (j, c))),
        out_shape=jax.ShapeDtypeStruct((n_tiles * tm, f), BF16),
        compiler_params=_params(2),
        name="moe_up",
    )(tile_expert, n_active, a, wg, wu)
    down = pl.pallas_call(
        functools.partial(_moe_down_kernel, tk=tk),
        grid_spec=pltpu.PrefetchScalarGridSpec(
            num_scalar_prefetch=2,
            grid=(d // tn, n_tiles),
            in_specs=[pl.BlockSpec((tm, f), lambda c, j, te, na: (row_tile(j, na), 0)),
                      pl.BlockSpec((None, None, f, tn), lambda c, j, te, na: (layer, te[j], 0, c))],
            out_specs=pl.BlockSpec((tm, tn), lambda c, j, te, na: (j, c))),
        out_shape=jax.ShapeDtypeStruct((n_tiles * tm, d), F32),
        compiler_params=_params(2),
        name="moe_down",
    )(tile_expert, n_active, up, wd)
    g_out = jnp.ones((d,), F32) if out_g is None else out_g
    return pl.pallas_call(
        functools.partial(_combine_kernel, apply_norm=out_g is not None),
        grid_spec=pltpu.PrefetchScalarGridSpec(
            num_scalar_prefetch=1,
            grid=(t // tt,),
            in_specs=[pl.BlockSpec((tt, d), lambda i, dst: (i, 0)),
                      pl.BlockSpec((tt, LANES), lambda i, dst: (i, 0)),
                      pl.BlockSpec((1, d), lambda i, dst: (0, 0)),
                      pl.BlockSpec(memory_space=pl.ANY)],
            out_specs=pl.BlockSpec((tt, d), lambda i, dst: (i, 0)),
            scratch_shapes=[pltpu.VMEM((TOP_K, tt, d), F32),
                            pltpu.SemaphoreType.DMA(())]),
        out_shape=jax.ShapeDtypeStruct((t, d), F32),
        compiler_params=_params(1),
        name="moe_combine",
    )(dest, x, gate, g_out.reshape(1, d), down)


def _mixer(x, mem_n, b, s, layer, w_in, rpb, w_mem_kv, w_br_a, w_br_b, w_br_c, w_br_m, w_gate, b_gate, w_out,
           norm_g):
    t = b * s
    h = rmsnorm(x, norm_g, BF16)
    n_main = 3 * NA_W + 3 * NA_W + FN_W
    z = matmul(h, w_in, layer, tm=1024, tn=NA_W, n_cols=n_main, name="in_proj").reshape(b, s, n_main)
    w_qm = _pad_heads(w_in[layer, :, n_main:], 1)[None]
    q_m = matmul(h, w_qm, 0, tm=1024, tn=512, name="in_proj_mem")
    kv_m = matmul(mem_n, _pad_heads(w_mem_kv[layer], 1)[None], 0, tm=mem_n.shape[0], tn=512, name="mem_kv_proj")
    y_a = neighbourhood_attention(z, rpb).reshape(t, NA_W)
    y_b = fourier_mix(z).reshape(t, FN_W)
    y_c = dilated_attention(z)
    y_m = memory_attention(q_m.reshape(b, s, MEM_W_PAD), kv_m.reshape(b, -1, 2 * MEM_W_PAD)).reshape(t, MEM_W_PAD)
    merged = gated_merge(h, (y_a, y_b, y_c, y_m), w_gate, b_gate,
                         (w_br_a, w_br_b, w_br_c, _pad_heads(w_br_m[layer], 0)[None]),
                         (layer, layer, layer, layer, 0))
    return matmul(merged, w_out, layer, tm=1024, tn=512, out_dtype=F32, residual=x, name="out_proj")


def kernel(x, mem, norm_mix, w_in, rpb, norm_mem, w_mem_kv, w_br_a, w_br_b, w_br_c, w_br_m, w_gate, b_gate,
           w_out, norm_ffn, dense_w_gate, dense_w_up, dense_w_down, router, exp_w_gate, exp_w_up, exp_w_down,
           final_norm):
    b, s, d = x.shape
    depth = norm_mix.shape[0]
    t = b * s
    xf = x.reshape(t, d)
    memf = mem.reshape(-1, d)
    normed = False
    for layer in range(depth):
        mem_n = rmsnorm(memf, norm_mem[layer], BF16)
        xf = _mixer(xf, mem_n, b, s, layer, w_in, rpb[layer], w_mem_kv, w_br_a, w_br_b, w_br_c, w_br_m, w_gate,
                    b_gate, w_out, norm_mix[layer])
        i = layer // 2
        if layer % 2 == 0:
            h = rmsnorm(xf, norm_ffn[layer], BF16)
            u = swiglu_up(h, dense_w_gate, dense_w_up, i)
            xf = matmul(u, dense_w_down, i, tm=1024, tn=256, out_dtype=F32, residual=xf, name="swiglu_down")
        else:
            normed = layer == depth - 1
            xf = moe_layer(xf, norm_ffn[layer], router[i], exp_w_gate, exp_w_up, exp_w_down, i,
                           final_norm if normed else None)
    if not normed:
        xf = rmsnorm(xf, final_norm, F32)
    return xf.reshape(b, s, d)
```

```python
import functools

import numpy as np
import jax
import jax.numpy as jnp
from jax import lax
from jax.experimental import pallas as pl
from jax.experimental.pallas import tpu as pltpu

F32 = jnp.float32
BF16 = jnp.bfloat16

D_MODEL = 2048
GRID_W = 64
HEAD_DIM = 128
RMS_EPS = 1e-6
NA_HEADS = 6
NA_KH = 8
NA_KW = 16
NA_W = NA_HEADS * HEAD_DIM
FN_GROUPS = 6
FN_GROUP_W = 128
FN_W = FN_GROUPS * FN_GROUP_W
DIL_DILATIONS = (1, 4, 16)
DIL_HALF_WINDOW = 64
DIL_HEADS_PER_GROUP = 2
DIL_GROUP_W = DIL_HEADS_PER_GROUP * HEAD_DIM
Q_BLOCK = 128
ROT_DIM = HEAD_DIM // 4
ROPE_THETA = 500000.0
MEM_HEADS = 4
MEM_HEAD_DIM = 192
MEM_HEAD_PAD = 256
MEM_W = MEM_HEADS * MEM_HEAD_DIM
MEM_W_PAD = MEM_HEADS * MEM_HEAD_PAD
N_EXPERTS = 8
TOP_K = 2
MASK_VALUE = -1e30

V7X_VMEM_BYTES = 64 * 1024 * 1024
VMEM_LIMIT_BYTES = V7X_VMEM_BYTES - 4 * 1024 * 1024
LANES = 128


def _params(n_axes):
    return pltpu.CompilerParams(dimension_semantics=("arbitrary",) * n_axes,
                                vmem_limit_bytes=VMEM_LIMIT_BYTES)


def _dot(a, b):
    return jnp.dot(a, b, preferred_element_type=F32)


def _dot_nt(a, b):
    return lax.dot_general(a, b, (((1,), (1,)), ((), ())), preferred_element_type=F32)


def _rmsnorm_body(x, g):
    return x * lax.rsqrt(jnp.mean(x * x, axis=-1, keepdims=True) + RMS_EPS) * g


def _rmsnorm_kernel(x_ref, g_ref, o_ref):
    o_ref[...] = _rmsnorm_body(x_ref[...], g_ref[...]).astype(o_ref.dtype)


def rmsnorm(x, g, out_dtype, tm=512):
    m, d = x.shape
    return pl.pallas_call(
        _rmsnorm_kernel,
        grid=(m // tm,),
        in_specs=[pl.BlockSpec((tm, d), lambda i: (i, 0)),
                  pl.BlockSpec((1, d), lambda i: (0, 0))],
        out_specs=pl.BlockSpec((tm, d), lambda i: (i, 0)),
        out_shape=jax.ShapeDtypeStruct((m, d), out_dtype),
        compiler_params=_params(1),
        name="rmsnorm",
    )(x, g.reshape(1, d))


def _mm_kernel(a_ref, w_ref, o_ref):
    o_ref[...] = _dot(a_ref[...], w_ref[...].astype(BF16)).astype(o_ref.dtype)


def _mm_res_kernel(a_ref, w_ref, r_ref, o_ref):
    o_ref[...] = (r_ref[...] + _dot(a_ref[...], w_ref[...].astype(BF16))).astype(o_ref.dtype)


def matmul(a, w, layer, *, tm, tn, n_cols=None, out_dtype=BF16, residual=None, name="matmul"):
    m, k = a.shape
    n = w.shape[2] if n_cols is None else n_cols
    in_specs = [pl.BlockSpec((tm, k), lambda i, j: (i, 0)),
                pl.BlockSpec((None, k, tn), lambda i, j: (layer, 0, j))]
    args = [a, w]
    kernel = _mm_kernel
    if residual is not None:
        in_specs.append(pl.BlockSpec((tm, tn), lambda i, j: (i, j)))
        args.append(residual)
        kernel = _mm_res_kernel
    return pl.pallas_call(
        kernel,
        grid=(m // tm, n // tn),
        in_specs=in_specs,
        out_specs=pl.BlockSpec((tm, tn), lambda i, j: (i, j)),
        out_shape=jax.ShapeDtypeStruct((m, n), out_dtype),
        compiler_params=_params(2),
        name=name,
    )(*args)


NA_Q_ROWS = 4
NA_KEY_ROWS = NA_KH + NA_Q_ROWS


def _na_key_row_start(step, rows):
    return np.clip(step * NA_Q_ROWS - NA_KH // 2, 0, rows - NA_KEY_ROWS)


def _na_bias_table(rpb, rows):
    col = np.arange(GRID_W)
    col_start = np.clip(col - NA_KW // 2, 0, GRID_W - NA_KW)
    rel_col = col[None, :] - col[:, None] + (NA_KW - 1)
    col_ok = (col[None, :] >= col_start[:, None]) & (col[None, :] < col_start[:, None] + NA_KW)
    slab = rpb[:, :, np.clip(rel_col, 0, 2 * NA_KW - 2)].astype(F32)
    n_steps = rows // NA_Q_ROWS
    rel_row = np.zeros((n_steps, NA_Q_ROWS, NA_KEY_ROWS), np.int64)
    row_ok = np.zeros((n_steps, NA_Q_ROWS, NA_KEY_ROWS), bool)
    for step in range(n_steps):
        for i in range(NA_Q_ROWS):
            r = step * NA_Q_ROWS + i
            win = np.clip(r - NA_KH // 2, 0, rows - NA_KH)
            key_row = _na_key_row_start(step, rows) + np.arange(NA_KEY_ROWS)
            row_ok[step, i] = (key_row >= win) & (key_row < win + NA_KH)
            rel_row[step, i] = key_row - r + (NA_KH - 1)
    assert all((rel_row[s_] * row_ok[s_] == rel_row[1] * row_ok[1]).all() and (row_ok[s_] == row_ok[1]).all()
               for s_ in range(1, n_steps - 1))
    cases = [0, 1, n_steps - 1]
    rel = np.clip(rel_row[cases], 0, 2 * NA_KH - 2)
    ok = row_ok[cases][:, None, :, None, :, None] & col_ok[None, None, None, :, None, :]
    bias = slab[:, rel]
    bias = bias.transpose(1, 0, 2, 4, 3, 5)
    bias = jnp.where(ok, bias, MASK_VALUE)
    return bias.reshape(3, NA_HEADS, NA_Q_ROWS * GRID_W, NA_KEY_ROWS * GRID_W)


def _na_kernel(q_ref, k_ref, v_ref, b_ref, o_ref, *, rows):
    step = pl.program_id(1)
    key_row0 = jnp.clip(step * NA_Q_ROWS - NA_KH // 2, 0, rows - NA_KEY_ROWS)
    k0 = pl.multiple_of(key_row0 * GRID_W, GRID_W)
    n_keys = NA_KEY_ROWS * GRID_W
    scale = HEAD_DIM ** -0.5
    for h in range(NA_HEADS):
        cols = slice(h * HEAD_DIM, (h + 1) * HEAD_DIM)
        k = k_ref[0, pl.ds(k0, n_keys), cols]
        v = v_ref[0, pl.ds(k0, n_keys), cols]
        s = _dot_nt(q_ref[0, :, cols], k) * scale + b_ref[0, h]
        p = jnp.exp(s - jnp.max(s, axis=-1, keepdims=True))
        den = jnp.sum(p, axis=-1, keepdims=True)
        o_ref[0, :, cols] = (_dot(p.astype(BF16), v) / den).astype(o_ref.dtype)


def neighbourhood_attention(z, rpb):
    b, s, _ = z.shape
    rows = s // GRID_W
    n_steps = rows // NA_Q_ROWS
    bias = _na_bias_table(rpb, rows)
    tq = NA_Q_ROWS * GRID_W
    bias_case = lambda i, r: (jnp.where(r == 0, 0, jnp.where(r == n_steps - 1, 2, 1)), 0, 0, 0)
    return pl.pallas_call(
        functools.partial(_na_kernel, rows=rows),
        grid=(b, n_steps),
        in_specs=[pl.BlockSpec((1, tq, NA_W), lambda i, r: (i, r, 0)),
                  pl.BlockSpec((1, s, NA_W), lambda i, r: (i, 0, 1)),
                  pl.BlockSpec((1, s, NA_W), lambda i, r: (i, 0, 2)),
                  pl.BlockSpec((1,) + bias.shape[1:], bias_case)],
        out_specs=pl.BlockSpec((1, tq, NA_W), lambda i, r: (i, r, 0)),
        out_shape=jax.ShapeDtypeStruct((b, s, NA_W), BF16),
        compiler_params=_params(2),
        name="neighbourhood_attention",
    )(z, z, z, bias)


def _dft_tables(s):
    n = np.arange(s, dtype=np.int64)
    ang = 2.0 * np.pi * ((n[:, None] * n[None, :]) % s) / s
    pos = np.concatenate([np.cos(ang), -np.sin(ang)], axis=1)
    c = np.arange(FN_GROUP_W, dtype=np.int64)
    ang_c = 2.0 * np.pi * ((c[:, None] * c[None, :]) % FN_GROUP_W) / FN_GROUP_W
    chan = np.concatenate([np.cos(ang_c), np.sin(ang_c)], axis=1)
    return jnp.asarray(pos, F32).astype(BF16), jnp.asarray(chan, F32).astype(BF16)


def _fourier_kernel(u_ref, chan_ref, pos_ref, o_ref, v_scr, *, s, norm):
    @pl.when(pl.program_id(1) == 0)
    def _():
        for g in range(FN_GROUPS):
            cols = slice(g * FN_GROUP_W, (g + 1) * FN_GROUP_W)
            t = _dot(u_ref[0, :, cols], chan_ref[...])
            v_scr[0:s, cols] = t[:, :FN_GROUP_W].astype(BF16)
            v_scr[s:2 * s, cols] = t[:, FN_GROUP_W:].astype(BF16)

    o_ref[0] = (_dot(pos_ref[...], v_scr[...]) * norm).astype(o_ref.dtype)


def fourier_mix(z, ts=512):
    b, s, _ = z.shape
    pos_tab, chan_tab = _dft_tables(s)
    norm = float((s * FN_GROUP_W) ** -0.5)
    return pl.pallas_call(
        functools.partial(_fourier_kernel, s=s, norm=norm),
        grid=(b, s // ts),
        in_specs=[pl.BlockSpec((1, s, FN_W), lambda i, j: (i, 0, 6)),
                  pl.BlockSpec(chan_tab.shape, lambda i, j: (0, 0)),
                  pl.BlockSpec((ts, 2 * s), lambda i, j: (j, 0))],
        out_specs=pl.BlockSpec((1, ts, FN_W), lambda i, j: (i, j, 0)),
        out_shape=jax.ShapeDtypeStruct((b, s, FN_W), BF16),
        scratch_shapes=[pltpu.VMEM((2 * s, FN_W), BF16)],
        compiler_params=_params(2),
        name="fourier_mix",
    )(z, chan_tab, pos_tab)


def _rope_tables(s):
    half = ROT_DIM // 2
    inv_freq = ROPE_THETA ** (-2.0 * jnp.arange(half, dtype=F32) / ROT_DIM)
    ang = jnp.arange(s, dtype=jnp.int32).astype(F32)[:, None] * inv_freq[None, :]
    cos, sin = jnp.cos(ang), jnp.sin(ang)
    ones = jnp.ones((s, HEAD_DIM - ROT_DIM), F32)
    zeros_half = jnp.zeros((s, half), F32)
    zeros_rest = jnp.zeros((s, HEAD_DIM - ROT_DIM), F32)
    cos_t = jnp.concatenate([cos, cos, ones], axis=1)
    sin_lo = jnp.concatenate([-sin, zeros_half, zeros_rest], axis=1)
    sin_hi = jnp.concatenate([zeros_half, sin, zeros_rest], axis=1)
    return cos_t, sin_lo, sin_hi


def _dilated_kernel(q0, q1, q2, k0, k1, k2, v0, v1, v2, cos_ref, slo_ref, shi_ref, y_ref,
                    rq_ref, rk_ref, rv_ref, o_scr, l_scr, *, seq):
    half = ROT_DIM // 2
    scale = HEAD_DIM ** -0.5
    chunk = 4 * Q_BLOCK
    for g, (q_ref, k_ref, v_ref) in enumerate(((q0, k0, v0), (q1, k1, v1), (q2, k2, v2))):
        for c0 in range(0, seq, chunk):
            rows = slice(c0, c0 + chunk)
            cs, lo, hi = cos_ref[rows, :], slo_ref[rows, :], shi_ref[rows, :]
            for src, dst in ((q_ref, rq_ref), (k_ref, rk_ref)):
                x = src[0, rows, :].astype(F32)
                dst[g, rows, :] = x * cs + pltpu.roll(x, HEAD_DIM - half, 1) * lo + pltpu.roll(x, half, 1) * hi
            rv_ref[g, rows, :] = v_ref[0, rows, :].astype(F32)

    masks = {}

    def window_mask(n_keys, offset):
        if (n_keys, offset) not in masks:
            rel = (lax.broadcasted_iota(jnp.int32, (Q_BLOCK, n_keys), 1)
                   - lax.broadcasted_iota(jnp.int32, (Q_BLOCK, n_keys), 0) + offset)
            masks[(n_keys, offset)] = jnp.abs(rel) <= DIL_HALF_WINDOW
        return masks[(n_keys, offset)]

    for g, d in enumerate(DIL_DILATIONS):
        length = seq // d
        n_keys = min(length, Q_BLOCK + 2 * DIL_HALF_WINDOW)
        for rho in range(d):
            for m_q in range(0, length, Q_BLOCK):
                m_k = min(max(m_q - DIL_HALF_WINDOW, 0), length - n_keys)
                q_rows = pl.ds(rho + d * m_q, Q_BLOCK, stride=d) if d > 1 else pl.ds(m_q, Q_BLOCK)
                k_rows = pl.ds(rho + d * m_k, n_keys, stride=d) if d > 1 else pl.ds(m_k, n_keys)
                q = rq_ref[g, q_rows, :].astype(BF16)
                k = rk_ref[g, k_rows, :].astype(BF16)
                v = rv_ref[g, k_rows, :].astype(BF16)
                s = jnp.where(window_mask(n_keys, m_k - m_q), _dot_nt(q, k) * scale, MASK_VALUE)
                m = jnp.max(s, axis=-1, keepdims=True)
                p = jnp.exp(s - m)
                den = jnp.sum(p, axis=-1, keepdims=True)
                o_scr[g, q_rows, :] = _dot(p.astype(BF16), v) / den
                l_scr[g, q_rows, :] = jnp.broadcast_to(m + jnp.log(den), (Q_BLOCK, HEAD_DIM))

    n_groups = len(DIL_DILATIONS)
    for c0 in range(0, seq, chunk):
        rows = slice(c0, c0 + chunk)
        lses = [l_scr[g, rows, :] for g in range(n_groups)]
        top = functools.reduce(jnp.maximum, lses)
        ws = [jnp.exp(l - top) for l in lses]
        num = sum(w * o_scr[g, rows, :] for g, w in enumerate(ws))
        y_ref[0, rows, :] = (num / sum(ws)).astype(y_ref.dtype)


def dilated_attention(z):
    b, s, _ = z.shape
    n_groups = len(DIL_DILATIONS)
    tables = _rope_tables(s)
    heads_per_block = NA_W // HEAD_DIM

    def head_spec(t, g):
        first = (3 + t) * heads_per_block + g * DIL_HEADS_PER_GROUP
        return pl.BlockSpec((1, s, HEAD_DIM), lambda i, h: (i, 0, first + h))

    table_spec = pl.BlockSpec((s, HEAD_DIM), lambda i, h: (0, 0))
    scratch = pltpu.VMEM((n_groups, s, HEAD_DIM), F32)
    return pl.pallas_call(
        functools.partial(_dilated_kernel, seq=s),
        grid=(b, DIL_HEADS_PER_GROUP),
        in_specs=[head_spec(t, g) for t in range(3) for g in range(n_groups)] + [table_spec] * 3,
        out_specs=pl.BlockSpec((1, s, HEAD_DIM), lambda i, h: (i, 0, h)),
        out_shape=jax.ShapeDtypeStruct((b, s, DIL_GROUP_W), BF16),
        scratch_shapes=[scratch] * 5,
        compiler_params=_params(2),
        name="dilated_attention",
    )(*([z] * 9), *tables).reshape(b * s, DIL_GROUP_W)


def _mem_attn_kernel(q_ref, k_ref, v_ref, o_ref):
    scale = MEM_HEAD_DIM ** -0.5
    for h in range(MEM_HEADS):
        cols = slice(h * MEM_HEAD_PAD, (h + 1) * MEM_HEAD_PAD)
        s = _dot_nt(q_ref[0, :, cols], k_ref[0, :, cols]) * scale
        p = jnp.exp(s - jnp.max(s, axis=-1, keepdims=True))
        den = jnp.sum(p, axis=-1, keepdims=True)
        o_ref[0, :, cols] = (_dot(p.astype(BF16), v_ref[0, :, cols]) / den).astype(o_ref.dtype)


def memory_attention(q, kv, tq=512):
    b, s, _ = q.shape
    m = kv.shape[1]
    return pl.pallas_call(
        _mem_attn_kernel,
        grid=(b, s // tq),
        in_specs=[pl.BlockSpec((1, tq, MEM_W_PAD), lambda i, j: (i, j, 0)),
                  pl.BlockSpec((1, m, MEM_W_PAD), lambda i, j: (i, 0, 0)),
                  pl.BlockSpec((1, m, MEM_W_PAD), lambda i, j: (i, 0, 1))],
        out_specs=pl.BlockSpec((1, tq, MEM_W_PAD), lambda i, j: (i, j, 0)),
        out_shape=jax.ShapeDtypeStruct((b, s, MEM_W_PAD), BF16),
        compiler_params=_params(2),
        name="memory_attention",
    )(q, kv, kv)


def _pad_heads(w, axis):
    shape = list(w.shape)
    n = shape[axis] // MEM_HEAD_DIM
    w = w.reshape(shape[:axis] + [n, MEM_HEAD_DIM] + shape[axis + 1:])
    pad = [(0, 0)] * w.ndim
    pad[axis + 1] = (0, MEM_HEAD_PAD - MEM_HEAD_DIM)
    w = jnp.pad(w, pad)
    return w.reshape(shape[:axis] + [n * MEM_HEAD_PAD] + shape[axis + 1:])


def _merge_kernel(h_ref, ya_ref, yb_ref, yc_ref, ym_ref, wg0, wg1, wg2, wg3, bg0, bg1, bg2, bg3,
                  wa_ref, wb_ref, wc_ref, wm_ref, o_ref):
    h = h_ref[...]
    acc = None
    for y_ref, wg, bg, wbr in ((ya_ref, wg0, bg0, wa_ref), (yb_ref, wg1, bg1, wb_ref),
                               (yc_ref, wg2, bg2, wc_ref), (ym_ref, wg3, bg3, wm_ref)):
        gate = jax.nn.sigmoid(_dot(h, wg[...].astype(BF16)) + bg[...])
        term = gate * _dot(y_ref[...], wbr[...].astype(BF16))
        acc = term if acc is None else acc + term
    o_ref[...] = acc.astype(o_ref.dtype)


def _gate_block(i, j, *, layer, branch, n_tiles):
    return (layer, 0, branch * n_tiles + j)


def gated_merge(h, ys, w_gate, b_gate, w_brs, layers, tm=1024, tn=256):
    m, d = h.shape
    n_tiles = d // tn
    row = lambda width: pl.BlockSpec((tm, width), lambda i, j: (i, 0))
    gate_maps = [functools.partial(_gate_block, layer=layers[0], branch=br, n_tiles=n_tiles) for br in range(4)]
    gate_w = [pl.BlockSpec((None, d, tn), gm) for gm in gate_maps]
    gate_b = [pl.BlockSpec((None, 1, tn), gm) for gm in gate_maps]
    br_w = [pl.BlockSpec((None, w.shape[1], tn), functools.partial(lambda i, j, l: (l, 0, j), l=l))
            for w, l in zip(w_brs, layers[1:])]
    b_gate3 = b_gate.reshape(b_gate.shape[0], 1, -1)
    return pl.pallas_call(
        _merge_kernel,
        grid=(m // tm, n_tiles),
        in_specs=[row(d)] + [row(y.shape[1]) for y in ys] + gate_w + gate_b + br_w,
        out_specs=pl.BlockSpec((tm, tn), lambda i, j: (i, j)),
        out_shape=jax.ShapeDtypeStruct((m, d), BF16),
        compiler_params=_params(2),
        name="gated_merge",
    )(h, *ys, w_gate, w_gate, w_gate, w_gate, b_gate3, b_gate3, b_gate3, b_gate3, *w_brs)


def _silu_mul(g, u):
    return g * jax.nn.sigmoid(g) * u


def _swiglu_up_kernel(a_ref, wg_ref, wu_ref, o_ref):
    a = a_ref[...]
    o_ref[...] = _silu_mul(_dot(a, wg_ref[...].astype(BF16)), _dot(a, wu_ref[...].astype(BF16))).astype(o_ref.dtype)


def swiglu_up(a, wg, wu, layer, tm=1024, tf=512):
    m, k = a.shape
    f = wg.shape[2]
    return pl.pallas_call(
        _swiglu_up_kernel,
        grid=(m // tm, f // tf),
        in_specs=[pl.BlockSpec((tm, k), lambda i, j: (i, 0)),
                  pl.BlockSpec((None, k, tf), lambda i, j: (layer, 0, j)),
                  pl.BlockSpec((None, k, tf), lambda i, j: (layer, 0, j))],
        out_specs=pl.BlockSpec((tm, tf), lambda i, j: (i, j)),
        out_shape=jax.ShapeDtypeStruct((m, f), BF16),
        compiler_params=_params(2),
        name="swiglu_up",
    )(a, wg, wu)


def _router_kernel(x_ref, g_ref, r_ref, idx_ref, rank_ref, gate_ref, cnt_ref):
    @pl.when(pl.program_id(0) == 0)
    def _():
        cnt_ref[...] = jnp.zeros_like(cnt_ref)

    h = _rmsnorm_body(x_ref[...], g_ref[...])
    logits = jnp.dot(h, r_ref[...], preferred_element_type=F32, precision=lax.Precision.HIGHEST)
    lane = lax.broadcasted_iota(jnp.int32, logits.shape, 1).astype(F32)
    neg = -jnp.inf
    lg = jnp.where(lane < N_EXPERTS, logits, neg)
    v1 = jnp.max(lg, axis=-1, keepdims=True)
    i1 = jnp.min(jnp.where(lg == v1, lane, float(LANES)), axis=-1, keepdims=True)
    lg2 = jnp.where(lane == i1, neg, lg)
    v2 = jnp.max(lg2, axis=-1, keepdims=True)
    i2 = jnp.min(jnp.where(lg2 == v2, lane, float(LANES)), axis=-1, keepdims=True)
    e = jnp.exp(v2 - v1)
    g1 = 1.0 / (1.0 + e)
    g2 = e / (1.0 + e)
    onehot = jnp.where((lane == i1) | (lane == i2), 1.0, 0.0)
    tt = onehot.shape[0]
    earlier = lax.broadcasted_iota(jnp.int32, (tt, tt), 1) < lax.broadcasted_iota(jnp.int32, (tt, tt), 0)
    pos = _dot(jnp.where(earlier, 1.0, 0.0).astype(BF16), onehot.astype(BF16)) + cnt_ref[...]
    r1 = jnp.sum(jnp.where(lane == i1, pos, 0.0), axis=-1, keepdims=True)
    r2 = jnp.sum(jnp.where(lane == i2, pos, 0.0), axis=-1, keepdims=True)
    cnt_ref[...] += jnp.sum(onehot, axis=0, keepdims=True)
    idx_ref[...] = jnp.where(lane == 0, i1, jnp.where(lane == 1, i2, 0.0)).astype(jnp.int32)
    rank_ref[...] = jnp.where(lane == 0, r1, jnp.where(lane == 1, r2, 0.0)).astype(jnp.int32)
    gate_ref[...] = jnp.where(lane == 0, g1, jnp.where(lane == 1, g2, 0.0))


def route(x, g, router, tm=512):
    m, d = x.shape
    r_pad = jnp.pad(router, ((0, 0), (0, LANES - N_EXPERTS)))
    lane_spec = pl.BlockSpec((tm, LANES), lambda i: (i, 0))
    return pl.pallas_call(
        _router_kernel,
        grid=(m // tm,),
        in_specs=[pl.BlockSpec((tm, d), lambda i: (i, 0)),
                  pl.BlockSpec((1, d), lambda i: (0, 0)),
                  pl.BlockSpec((d, LANES), lambda i: (0, 0))],
        out_specs=[lane_spec, lane_spec, lane_spec, pl.BlockSpec((1, LANES), lambda i: (0, 0))],
        out_shape=[jax.ShapeDtypeStruct((m, LANES), jnp.int32),
                   jax.ShapeDtypeStruct((m, LANES), jnp.int32),
                   jax.ShapeDtypeStruct((m, LANES), F32),
                   jax.ShapeDtypeStruct((1, LANES), F32)],
        compiler_params=_params(1),
        name="route",
    )(x, g.reshape(1, d), r_pad)


def _dispatch_kernel(dest_ref, lo_ref, hi_ref, na_ref, x_hbm, g_ref, a_ref, src_ref, rows_ref, sems):
    j = pl.program_id(0)
    tm = rows_ref.shape[1]
    n_active = na_ref[0]

    def row_copy(tile, r):
        slot = lax.rem(tile, 2)
        return pltpu.make_async_copy(x_hbm.at[pl.ds(src_ref[tile * tm + r], 1)],
                                     rows_ref.at[slot, pl.ds(r, 1)], sems.at[slot])

    def start_tile(tile):
        def start(r, c):
            row_copy(tile, r).start()
            return c

        lax.fori_loop(0, tm, start, 0, unroll=8)

    @pl.when(j == 0)
    def _():
        def clear(i, c):
            src_ref[i] = 0
            return c

        for e in range(N_EXPERTS):
            lax.fori_loop(lo_ref[e], hi_ref[e], clear, 0)

        def fill(i, c):
            src_ref[dest_ref[i]] = lax.shift_right_logical(i, 1)
            return c

        lax.fori_loop(0, dest_ref.shape[0], fill, 0, unroll=8)

        @pl.when(n_active > 0)
        def _():
            start_tile(0)

    @pl.when(j + 1 < n_active)
    def _():
        start_tile(j + 1)

    @pl.when(j < n_active)
    def _():
        def wait(r, c):
            row_copy(j, r).wait()
            return c

        lax.fori_loop(0, tm, wait, 0, unroll=8)
        a_ref[...] = _rmsnorm_body(rows_ref[lax.rem(j, 2)], g_ref[...]).astype(a_ref.dtype)

    @pl.when(j >= n_active)
    def _():
        a_ref[...] = jnp.zeros_like(a_ref)


def _combine_kernel(dest_ref, x_ref, gate_ref, g_ref, y_hbm, o_ref, rows_ref, sems, *, apply_norm):
    tt = x_ref.shape[0]
    i = pl.program_id(0)

    def row_copy(tile, r, k):
        slot = lax.rem(tile, 2)
        row = dest_ref[TOP_K * (tile * tt + r) + k]
        return pltpu.make_async_copy(y_hbm.at[pl.ds(row, 1)], rows_ref.at[slot, k, pl.ds(r, 1)], sems.at[slot])

    def start_tile(tile):
        def start(r, c):
            for k in range(TOP_K):
                row_copy(tile, r, k).start()
            return c

        lax.fori_loop(0, tt, start, 0, unroll=4)

    @pl.when(i == 0)
    def _():
        start_tile(0)

    @pl.when(i + 1 < pl.num_programs(0))
    def _():
        start_tile(i + 1)

    def wait(r, c):
        for k in range(TOP_K):
            row_copy(i, r, k).wait()
        return c

    lax.fori_loop(0, tt, wait, 0, unroll=4)
    gates = gate_ref[...]
    slot = lax.rem(i, 2)
    y = x_ref[...]
    for k in range(TOP_K):
        y = y + gates[:, k:k + 1] * rows_ref[slot, k]
    o_ref[...] = _rmsnorm_body(y, g_ref[...]) if apply_norm else y


def _moe_up_kernel(te_ref, na_ref, a_ref, wg_ref, wu_ref, o_ref):
    @pl.when(pl.program_id(1) < na_ref[0])
    def _():
        a = a_ref[...]
        o_ref[...] = _silu_mul(_dot(a, wg_ref[...].astype(BF16)), _dot(a, wu_ref[...].astype(BF16))).astype(o_ref.dtype)

    @pl.when(pl.program_id(1) >= na_ref[0])
    def _():
        o_ref[...] = jnp.zeros_like(o_ref)


def _moe_down_kernel(te_ref, na_ref, a_ref, w_ref, o_ref, *, tk):
    @pl.when(pl.program_id(1) < na_ref[0])
    def _():
        acc = None
        for c in range(a_ref.shape[1] // tk):
            part = _dot(a_ref[:, c * tk:(c + 1) * tk], w_ref[c * tk:(c + 1) * tk, :].astype(BF16))
            acc = part if acc is None else acc + part
        o_ref[...] = acc

    @pl.when(pl.program_id(1) >= na_ref[0])
    def _():
        o_ref[...] = jnp.zeros_like(o_ref)


def moe_layer(x, norm_g, router, wg, wu, wd, layer, out_g, tm=512, tf=1024, tn=512, tk=1024, tt=256):
    assert TOP_K == 2
    t, d = x.shape
    _, n_exp, _, f = wg.shape
    n_rows = t * TOP_K
    n_tiles = n_rows // tm + n_exp
    idx, rank, gate, cnt = route(x, norm_g, router)
    counts = cnt[0, :n_exp].astype(jnp.int32)
    tiles_per = (counts + tm - 1) // tm
    tile_end = jnp.cumsum(tiles_per)
    start = (tile_end - tiles_per) * tm
    n_active = tile_end[-1:]
    tile_ids = jnp.arange(n_tiles, dtype=jnp.int32)
    tile_expert = jnp.sum((tile_ids[:, None] >= tile_end[None, :]).astype(jnp.int32), axis=1)
    last_expert = jnp.sum((n_active - 1 >= tile_end).astype(jnp.int32))
    tile_expert = jnp.where(tile_ids < n_active, tile_expert, last_expert).astype(jnp.int32)
    picked = idx[:, :TOP_K, None] == jnp.arange(n_exp, dtype=jnp.int32)
    dest = (jnp.sum(jnp.where(picked, start, 0), axis=-1) + rank[:, :TOP_K]).reshape(n_rows)

    row_tile = lambda j, na: jnp.maximum(jnp.minimum(j, na[0] - 1), 0)
    a = pl.pallas_call(
        _dispatch_kernel,
        grid_spec=pltpu.PrefetchScalarGridSpec(
            num_scalar_prefetch=4,
            grid=(n_tiles,),
            in_specs=[pl.BlockSpec(memory_space=pl.ANY),
                      pl.BlockSpec((1, d), lambda j, dst, lo, hi, na: (0, 0))],
            out_specs=pl.BlockSpec((tm, d), lambda j, dst, lo, hi, na: (j, 0)),
            scratch_shapes=[pltpu.SMEM((n_tiles * tm,), jnp.int32),
                            pltpu.VMEM((2, tm, d), F32),
                            pltpu.SemaphoreType.DMA((2,))]),
        out_shape=jax.ShapeDtypeStruct((n_tiles * tm, d), BF16),
        compiler_params=_params(1),
        name="moe_dispatch",
    )(dest, start + counts, tile_end * tm, n_active, x, norm_g.reshape(1, d))
    up = pl.pallas_call(
        _moe_up_kernel,
        grid_spec=pltpu.PrefetchScalarGridSpec(
            num_scalar_prefetch=2,
            grid=(f // tf, n_tiles),
            in_specs=[pl.BlockSpec((tm, d), lambda c, j, te, na: (row_tile(j, na), 0)),
                      pl.BlockSpec((None, None, d, tf), lambda c, j, te, na: (layer, te[j], 0, c)),
                      pl.BlockSpec((None, None, d, tf), lambda c, j, te, na: (layer, te[j], 0, c))],
            out_specs=pl.BlockSpec((tm, tf), lambda c, j, te, na: (j, c))),
        out_shape=jax.ShapeDtypeStruct((n_tiles * tm, f), BF16),
        compiler_params=_params(2),
        name="moe_up",
    )(tile_expert, n_active, a, wg, wu)
    down = pl.pallas_call(
        functools.partial(_moe_down_kernel, tk=tk),
        grid_spec=pltpu.PrefetchScalarGridSpec(
            num_scalar_prefetch=2,
            grid=(d // tn, n_tiles),
            in_specs=[pl.BlockSpec((tm, f), lambda c, j, te, na: (row_tile(j, na), 0)),
                      pl.BlockSpec((None, None, f, tn), lambda c, j, te, na: (layer, te[j], 0, c))],
            out_specs=pl.BlockSpec((tm, tn), lambda c, j, te, na: (j, c))),
        out_shape=jax.ShapeDtypeStruct((n_tiles * tm, d), F32),
        compiler_params=_params(2),
        name="moe_down",
    )(tile_expert, n_active, up, wd)
    g_out = jnp.ones((d,), F32) if out_g is None else out_g
    return pl.pallas_call(
        functools.partial(_combine_kernel, apply_norm=out_g is not None),
        grid_spec=pltpu.PrefetchScalarGridSpec(
            num_scalar_prefetch=1,
            grid=(t // tt,),
            in_specs=[pl.BlockSpec((tt, d), lambda i, dst: (i, 0)),
                      pl.BlockSpec((tt, LANES), lambda i, dst: (i, 0)),
                      pl.BlockSpec((1, d), lambda i, dst: (0, 0)),
                      pl.BlockSpec(memory_space=pl.ANY)],
            out_specs=pl.BlockSpec((tt, d), lambda i, dst: (i, 0)),
            scratch_shapes=[pltpu.VMEM((2, TOP_K, tt, d), F32),
                            pltpu.SemaphoreType.DMA((2,))]),
        out_shape=jax.ShapeDtypeStruct((t, d), F32),
        compiler_params=_params(1),
        name="moe_combine",
    )(dest, x, gate, g_out.reshape(1, d), down)


def _mixer(x, mem_n, b, s, layer, w_in, rpb, w_mem_kv, w_br_a, w_br_b, w_br_c, w_br_m, w_gate, b_gate, w_out,
           norm_g):
    t = b * s
    h = rmsnorm(x, norm_g, BF16)
    n_main = 3 * NA_W + 3 * NA_W + FN_W
    z = matmul(h, w_in, layer, tm=2048, tn=NA_W, n_cols=n_main, name="in_proj").reshape(b, s, n_main)
    w_qm = _pad_heads(w_in[layer, :, n_main:], 1)[None]
    q_m = matmul(h, w_qm, 0, tm=1024, tn=512, name="in_proj_mem")
    kv_m = matmul(mem_n, _pad_heads(w_mem_kv[layer], 1)[None], 0, tm=mem_n.shape[0], tn=512, name="mem_kv_proj")
    y_a = neighbourhood_attention(z, rpb).reshape(t, NA_W)
    y_b = fourier_mix(z).reshape(t, FN_W)
    y_c = dilated_attention(z)
    y_m = memory_attention(q_m.reshape(b, s, MEM_W_PAD), kv_m.reshape(b, -1, 2 * MEM_W_PAD)).reshape(t, MEM_W_PAD)
    merged = gated_merge(h, (y_a, y_b, y_c, y_m), w_gate, b_gate,
                         (w_br_a, w_br_b, w_br_c, _pad_heads(w_br_m[layer], 0)[None]),
                         (layer, layer, layer, layer, 0))
    return matmul(merged, w_out, layer, tm=2048, tn=512, out_dtype=F32, residual=x, name="out_proj")


def kernel(x, mem, norm_mix, w_in, rpb, norm_mem, w_mem_kv, w_br_a, w_br_b, w_br_c, w_br_m, w_gate, b_gate,
           w_out, norm_ffn, dense_w_gate, dense_w_up, dense_w_down, router, exp_w_gate, exp_w_up, exp_w_down,
           final_norm):
    b, s, d = x.shape
    depth = norm_mix.shape[0]
    t = b * s
    xf = x.reshape(t, d)
    memf = mem.reshape(-1, d)
    normed = False
    for layer in range(depth):
        mem_n = rmsnorm(memf, norm_mem[layer], BF16)
        xf = _mixer(xf, mem_n, b, s, layer, w_in, rpb[layer], w_mem_kv, w_br_a, w_br_b, w_br_c, w_br_m, w_gate,
                    b_gate, w_out, norm_mix[layer])
        i = layer // 2
        if layer % 2 == 0:
            h = rmsnorm(xf, norm_ffn[layer], BF16)
            u = swiglu_up(h, dense_w_gate, dense_w_up, i)
            xf = matmul(u, dense_w_down, i, tm=1024, tn=256, out_dtype=F32, residual=xf, name="swiglu_down")
        else:
            normed = layer == depth - 1
            xf = moe_layer(xf, norm_ffn[layer], router[i], exp_w_gate, exp_w_up, exp_w_down, i,
                           final_norm if normed else None)
    if not normed:
        xf = rmsnorm(xf, final_norm, F32)
    return xf.reshape(b, s, d)
```

```python
import functools

import numpy as np
import jax
import jax.numpy as jnp
from jax import lax
from jax.experimental import pallas as pl
from jax.experimental.pallas import tpu as pltpu

F32 = jnp.float32
BF16 = jnp.bfloat16

D_MODEL = 2048
GRID_W = 64
HEAD_DIM = 128
RMS_EPS = 1e-6
NA_HEADS = 6
NA_KH = 8
NA_KW = 16
NA_W = NA_HEADS * HEAD_DIM
FN_GROUPS = 6
FN_GROUP_W = 128
FN_W = FN_GROUPS * FN_GROUP_W
DIL_DILATIONS = (1, 4, 16)
DIL_HALF_WINDOW = 64
DIL_HEADS_PER_GROUP = 2
DIL_GROUP_W = DIL_HEADS_PER_GROUP * HEAD_DIM
Q_BLOCK = 128
ROT_DIM = HEAD_DIM // 4
ROPE_THETA = 500000.0
MEM_HEADS = 4
MEM_HEAD_DIM = 192
MEM_W = MEM_HEADS * MEM_HEAD_DIM
N_EXPERTS = 8
TOP_K = 2
MASK_VALUE = -1e30

V7X_VMEM_BYTES = 64 * 1024 * 1024
VMEM_LIMIT_BYTES = V7X_VMEM_BYTES - 4 * 1024 * 1024
LANES = 128


def _params(n_axes):
    return pltpu.CompilerParams(dimension_semantics=("arbitrary",) * n_axes,
                                vmem_limit_bytes=VMEM_LIMIT_BYTES)


def _dot(a, b):
    return jnp.dot(a, b, preferred_element_type=F32)


def _dot_nt(a, b):
    return lax.dot_general(a, b, (((1,), (1,)), ((), ())), preferred_element_type=F32)


def _rmsnorm_body(x, g):
    return x * lax.rsqrt(jnp.mean(x * x, axis=-1, keepdims=True) + RMS_EPS) * g


def _rmsnorm_kernel(x_ref, g_ref, o_ref):
    o_ref[...] = _rmsnorm_body(x_ref[...], g_ref[...]).astype(o_ref.dtype)


def rmsnorm(x, g, out_dtype, tm=512):
    m, d = x.shape
    return pl.pallas_call(
        _rmsnorm_kernel,
        grid=(m // tm,),
        in_specs=[pl.BlockSpec((tm, d), lambda i: (i, 0)),
                  pl.BlockSpec((1, d), lambda i: (0, 0))],
        out_specs=pl.BlockSpec((tm, d), lambda i: (i, 0)),
        out_shape=jax.ShapeDtypeStruct((m, d), out_dtype),
        compiler_params=_params(1),
        name="rmsnorm",
    )(x, g.reshape(1, d))


def _mm_kernel(a_ref, w_ref, o_ref):
    o_ref[...] = _dot(a_ref[...], w_ref[...].astype(BF16)).astype(o_ref.dtype)


def _mm_res_kernel(a_ref, w_ref, r_ref, o_ref):
    o_ref[...] = (r_ref[...] + _dot(a_ref[...], w_ref[...].astype(BF16))).astype(o_ref.dtype)


def matmul(a, w, layer, *, tm, tn, n_cols=None, out_dtype=BF16, residual=None, name="matmul"):
    m, k = a.shape
    n = w.shape[2] if n_cols is None else n_cols
    in_specs = [pl.BlockSpec((tm, k), lambda i, j: (i, 0)),
                pl.BlockSpec((None, k, tn), lambda i, j: (layer, 0, j))]
    args = [a, w]
    kernel = _mm_kernel
    if residual is not None:
        in_specs.append(pl.BlockSpec((tm, tn), lambda i, j: (i, j)))
        args.append(residual)
        kernel = _mm_res_kernel
    return pl.pallas_call(
        kernel,
        grid=(m // tm, n // tn),
        in_specs=in_specs,
        out_specs=pl.BlockSpec((tm, tn), lambda i, j: (i, j)),
        out_shape=jax.ShapeDtypeStruct((m, n), out_dtype),
        compiler_params=_params(2),
        name=name,
    )(*args)


NA_Q_ROWS = 4
NA_KEY_ROWS = NA_KH + NA_Q_ROWS


def _na_key_row_start(step, rows):
    return np.clip(step * NA_Q_ROWS - NA_KH // 2, 0, rows - NA_KEY_ROWS)


def _na_bias_table(rpb, rows):
    col = np.arange(GRID_W)
    col_start = np.clip(col - NA_KW // 2, 0, GRID_W - NA_KW)
    rel_col = col[None, :] - col[:, None] + (NA_KW - 1)
    col_ok = (col[None, :] >= col_start[:, None]) & (col[None, :] < col_start[:, None] + NA_KW)
    slab = rpb[:, :, np.clip(rel_col, 0, 2 * NA_KW - 2)].astype(F32)
    n_steps = rows // NA_Q_ROWS
    rel_row = np.zeros((n_steps, NA_Q_ROWS, NA_KEY_ROWS), np.int64)
    row_ok = np.zeros((n_steps, NA_Q_ROWS, NA_KEY_ROWS), bool)
    for step in range(n_steps):
        for i in range(NA_Q_ROWS):
            r = step * NA_Q_ROWS + i
            win = np.clip(r - NA_KH // 2, 0, rows - NA_KH)
            key_row = _na_key_row_start(step, rows) + np.arange(NA_KEY_ROWS)
            row_ok[step, i] = (key_row >= win) & (key_row < win + NA_KH)
            rel_row[step, i] = key_row - r + (NA_KH - 1)
    assert all((rel_row[s_] * row_ok[s_] == rel_row[1] * row_ok[1]).all() and (row_ok[s_] == row_ok[1]).all()
               for s_ in range(1, n_steps - 1))
    slab = jnp.where(col_ok, slab, MASK_VALUE)
    hidden = jnp.full(slab.shape[:1] + slab.shape[2:], MASK_VALUE, F32)
    cases = []
    for step in (0, 1, n_steps - 1):
        q_rows = [jnp.concatenate([slab[:, rel_row[step, i, k]] if row_ok[step, i, k] else hidden
                                   for k in range(NA_KEY_ROWS)], axis=-1) for i in range(NA_Q_ROWS)]
        cases.append(jnp.concatenate(q_rows, axis=-2))
    return jnp.stack(cases)


def _na_kernel(q_ref, k_ref, v_ref, b_ref, o_ref, *, rows):
    step = pl.program_id(1)
    key_row0 = jnp.clip(step * NA_Q_ROWS - NA_KH // 2, 0, rows - NA_KEY_ROWS)
    k0 = pl.multiple_of(key_row0 * GRID_W, GRID_W)
    n_keys = NA_KEY_ROWS * GRID_W
    scale = HEAD_DIM ** -0.5
    for h in range(NA_HEADS):
        cols = slice(h * HEAD_DIM, (h + 1) * HEAD_DIM)
        k = k_ref[0, pl.ds(k0, n_keys), cols]
        v = v_ref[0, pl.ds(k0, n_keys), cols]
        s = _dot_nt(q_ref[0, :, cols], k) * scale + b_ref[0, h]
        p = jnp.exp(s - jnp.max(s, axis=-1, keepdims=True))
        den = jnp.sum(p, axis=-1, keepdims=True)
        o_ref[0, :, cols] = (_dot(p.astype(BF16), v) / den).astype(o_ref.dtype)


def neighbourhood_attention(z, rpb):
    b, s, _ = z.shape
    rows = s // GRID_W
    n_steps = rows // NA_Q_ROWS
    bias = _na_bias_table(rpb, rows)
    tq = NA_Q_ROWS * GRID_W
    bias_case = lambda i, r: (jnp.where(r == 0, 0, jnp.where(r == n_steps - 1, 2, 1)), 0, 0, 0)
    return pl.pallas_call(
        functools.partial(_na_kernel, rows=rows),
        grid=(b, n_steps),
        in_specs=[pl.BlockSpec((1, tq, NA_W), lambda i, r: (i, r, 0)),
                  pl.BlockSpec((1, s, NA_W), lambda i, r: (i, 0, 1)),
                  pl.BlockSpec((1, s, NA_W), lambda i, r: (i, 0, 2)),
                  pl.BlockSpec((1,) + bias.shape[1:], bias_case)],
        out_specs=pl.BlockSpec((1, tq, NA_W), lambda i, r: (i, r, 0)),
        out_shape=jax.ShapeDtypeStruct((b, s, NA_W), BF16),
        compiler_params=_params(2),
        name="neighbourhood_attention",
    )(z, z, z, bias)


def _dft_tables(s):
    half = s // 2
    n = np.arange(half, dtype=np.int64)
    ang = 2.0 * np.pi * ((n[:, None] * n[None, :]) % half) / half
    pos = np.concatenate([np.cos(ang), -np.sin(ang)], axis=1)
    c = np.arange(FN_GROUP_W, dtype=np.int64)
    ang_c = 2.0 * np.pi * ((c[:, None] * c[None, :]) % FN_GROUP_W) / FN_GROUP_W
    chan = np.concatenate([np.cos(ang_c), np.sin(ang_c)], axis=1)
    tw = np.pi * n[:, None] / half * np.ones((1, FN_GROUP_W))
    return (jnp.asarray(pos, F32).astype(BF16), jnp.asarray(chan, F32).astype(BF16),
            jnp.asarray(np.cos(tw), F32), jnp.asarray(np.sin(tw), F32))


def _fourier_kernel(u_ref, chan_ref, twc_ref, tws_ref, pos_ref, o_ref, ve_scr, vo_scr, y_scr, *, s, norm):
    half = s // 2

    @pl.when(pl.program_id(1) == 0)
    def _():
        twc, tws = twc_ref[...], tws_ref[...]
        for g in range(FN_GROUPS):
            cols = slice(g * FN_GROUP_W, (g + 1) * FN_GROUP_W)
            t = _dot(u_ref[0, :, cols], chan_ref[...])
            top, bot = t[:half], t[half:]
            fold = top + bot
            diff = top - bot
            dc, ds = diff[:, :FN_GROUP_W], diff[:, FN_GROUP_W:]
            ve_scr[0:half, cols] = fold[:, :FN_GROUP_W].astype(BF16)
            ve_scr[half:s, cols] = fold[:, FN_GROUP_W:].astype(BF16)
            vo_scr[0:half, cols] = (twc * dc - tws * ds).astype(BF16)
            vo_scr[half:s, cols] = (tws * dc + twc * ds).astype(BF16)

    rows = pos_ref.shape[0]
    y_even = _dot(pos_ref[...], ve_scr[...]) * norm
    y_odd = _dot(pos_ref[...], vo_scr[...]) * norm
    for g in range(FN_GROUPS):
        cols = slice(g * FN_GROUP_W, (g + 1) * FN_GROUP_W)
        y_scr[g, pl.ds(0, rows, stride=2), :] = y_even[:, cols]
        y_scr[g, pl.ds(1, rows, stride=2), :] = y_odd[:, cols]
        o_ref[0, :, cols] = y_scr[g].astype(o_ref.dtype)


def fourier_mix(z, ts=512):
    b, s, _ = z.shape
    pos_tab, chan_tab, tw_cos, tw_sin = _dft_tables(s)
    norm = float((s * FN_GROUP_W) ** -0.5)
    whole = lambda a: pl.BlockSpec(a.shape, lambda i, j: (0, 0))
    return pl.pallas_call(
        functools.partial(_fourier_kernel, s=s, norm=norm),
        grid=(b, s // (2 * ts)),
        in_specs=[pl.BlockSpec((1, s, FN_W), lambda i, j: (i, 0, 6)),
                  whole(chan_tab), whole(tw_cos), whole(tw_sin),
                  pl.BlockSpec((ts, s), lambda i, j: (j, 0))],
        out_specs=pl.BlockSpec((1, 2 * ts, FN_W), lambda i, j: (i, j, 0)),
        out_shape=jax.ShapeDtypeStruct((b, s, FN_W), BF16),
        scratch_shapes=[pltpu.VMEM((s, FN_W), BF16), pltpu.VMEM((s, FN_W), BF16),
                        pltpu.VMEM((FN_GROUPS, 2 * ts, FN_GROUP_W), F32)],
        compiler_params=_params(2),
        name="fourier_mix",
    )(z, chan_tab, tw_cos, tw_sin, pos_tab)


def _rope_tables(s):
    half = ROT_DIM // 2
    inv_freq = ROPE_THETA ** (-2.0 * jnp.arange(half, dtype=F32) / ROT_DIM)
    ang = jnp.arange(s, dtype=jnp.int32).astype(F32)[:, None] * inv_freq[None, :]
    cos, sin = jnp.cos(ang), jnp.sin(ang)
    ones = jnp.ones((s, HEAD_DIM - ROT_DIM), F32)
    zeros_half = jnp.zeros((s, half), F32)
    zeros_rest = jnp.zeros((s, HEAD_DIM - ROT_DIM), F32)
    cos_t = jnp.concatenate([cos, cos, ones], axis=1)
    sin_lo = jnp.concatenate([-sin, zeros_half, zeros_rest], axis=1)
    sin_hi = jnp.concatenate([zeros_half, sin, zeros_rest], axis=1)
    return cos_t, sin_lo, sin_hi


def _dilated_kernel(q0, q1, q2, k0, k1, k2, v0, v1, v2, cos_ref, slo_ref, shi_ref, y_ref,
                    rq_ref, rk_ref, rv_ref, o_scr, l_scr, *, seq):
    half = ROT_DIM // 2
    scale = HEAD_DIM ** -0.5
    chunk = 4 * Q_BLOCK
    for g, (q_ref, k_ref, v_ref) in enumerate(((q0, k0, v0), (q1, k1, v1), (q2, k2, v2))):
        for c0 in range(0, seq, chunk):
            rows = slice(c0, c0 + chunk)
            cs, lo, hi = cos_ref[rows, :], slo_ref[rows, :], shi_ref[rows, :]
            for src, dst in ((q_ref, rq_ref), (k_ref, rk_ref)):
                x = src[0, rows, :].astype(F32)
                dst[g, rows, :] = x * cs + pltpu.roll(x, HEAD_DIM - half, 1) * lo + pltpu.roll(x, half, 1) * hi
            rv_ref[g, rows, :] = v_ref[0, rows, :].astype(F32)

    masks = {}

    def window_mask(n_keys, offset):
        if (n_keys, offset) not in masks:
            rel = (lax.broadcasted_iota(jnp.int32, (Q_BLOCK, n_keys), 1)
                   - lax.broadcasted_iota(jnp.int32, (Q_BLOCK, n_keys), 0) + offset)
            masks[(n_keys, offset)] = jnp.abs(rel) <= DIL_HALF_WINDOW
        return masks[(n_keys, offset)]

    for g, d in enumerate(DIL_DILATIONS):
        length = seq // d
        n_keys = min(length, Q_BLOCK + 2 * DIL_HALF_WINDOW)
        for rho in range(d):
            for m_q in range(0, length, Q_BLOCK):
                m_k = min(max(m_q - DIL_HALF_WINDOW, 0), length - n_keys)
                q_rows = pl.ds(rho + d * m_q, Q_BLOCK, stride=d) if d > 1 else pl.ds(m_q, Q_BLOCK)
                k_rows = pl.ds(rho + d * m_k, n_keys, stride=d) if d > 1 else pl.ds(m_k, n_keys)
                q = rq_ref[g, q_rows, :].astype(BF16)
                k = rk_ref[g, k_rows, :].astype(BF16)
                v = rv_ref[g, k_rows, :].astype(BF16)
                s = jnp.where(window_mask(n_keys, m_k - m_q), _dot_nt(q, k) * scale, MASK_VALUE)
                m = jnp.max(s, axis=-1, keepdims=True)
                p = jnp.exp(s - m)
                den = jnp.sum(p, axis=-1, keepdims=True)
                o_scr[g, q_rows, :] = _dot(p.astype(BF16), v) / den
                l_scr[g, q_rows, :] = jnp.broadcast_to(m + jnp.log(den), (Q_BLOCK, HEAD_DIM))

    n_groups = len(DIL_DILATIONS)
    for c0 in range(0, seq, chunk):
        rows = slice(c0, c0 + chunk)
        lses = [l_scr[g, rows, :] for g in range(n_groups)]
        top = functools.reduce(jnp.maximum, lses)
        ws = [jnp.exp(l - top) for l in lses]
        num = sum(w * o_scr[g, rows, :] for g, w in enumerate(ws))
        y_ref[0, rows, :] = (num / sum(ws)).astype(y_ref.dtype)


def dilated_attention(z):
    b, s, _ = z.shape
    n_groups = len(DIL_DILATIONS)
    tables = _rope_tables(s)
    heads_per_block = NA_W // HEAD_DIM

    def head_spec(t, g):
        first = (3 + t) * heads_per_block + g * DIL_HEADS_PER_GROUP
        return pl.BlockSpec((1, s, HEAD_DIM), lambda i, h: (i, 0, first + h))

    table_spec = pl.BlockSpec((s, HEAD_DIM), lambda i, h: (0, 0))
    scratch = pltpu.VMEM((n_groups, s, HEAD_DIM), F32)
    return pl.pallas_call(
        functools.partial(_dilated_kernel, seq=s),
        grid=(b, DIL_HEADS_PER_GROUP),
        in_specs=[head_spec(t, g) for t in range(3) for g in range(n_groups)] + [table_spec] * 3,
        out_specs=pl.BlockSpec((1, s, HEAD_DIM), lambda i, h: (i, 0, h)),
        out_shape=jax.ShapeDtypeStruct((b, s, DIL_GROUP_W), BF16),
        scratch_shapes=[scratch] * 5,
        compiler_params=_params(2),
        name="dilated_attention",
    )(*([z] * 9), *tables).reshape(b * s, DIL_GROUP_W)


def _mem_attn_kernel(q_ref, k_ref, v_ref, o_ref):
    scale = MEM_HEAD_DIM ** -0.5
    q, k, v = q_ref[0], k_ref[0], v_ref[0]
    col = lax.broadcasted_iota(jnp.int32, (1, MEM_W), 1)
    acc = None
    for h in range(MEM_HEADS):
        in_head = (col >= h * MEM_HEAD_DIM) & (col < (h + 1) * MEM_HEAD_DIM)
        s = _dot_nt(jnp.where(in_head, q, jnp.zeros_like(q)), k) * scale
        p = jnp.exp(s - jnp.max(s, axis=-1, keepdims=True))
        den = jnp.sum(p, axis=-1, keepdims=True)
        o = jnp.where(in_head, _dot(p.astype(BF16), v) / den, 0.0)
        acc = o if acc is None else acc + o
    o_ref[0] = acc.astype(o_ref.dtype)


def memory_attention(z, kv, tq=512):
    b, s, _ = z.shape
    m = kv.shape[1]
    return pl.pallas_call(
        _mem_attn_kernel,
        grid=(b, s // tq),
        in_specs=[pl.BlockSpec((1, tq, MEM_W), lambda i, j: (i, j, 7)),
                  pl.BlockSpec((1, m, MEM_W), lambda i, j: (i, 0, 0)),
                  pl.BlockSpec((1, m, MEM_W), lambda i, j: (i, 0, 1))],
        out_specs=pl.BlockSpec((1, tq, MEM_W), lambda i, j: (i, j, 0)),
        out_shape=jax.ShapeDtypeStruct((b, s, MEM_W), BF16),
        compiler_params=_params(2),
        name="memory_attention",
    )(z, kv, kv)


def _merge_kernel(h_ref, ya_ref, yb_ref, yc_ref, ym_ref, wg0, wg1, wg2, wg3, bg0, bg1, bg2, bg3,
                  wa_ref, wb_ref, wc_ref, wm_ref, o_ref):
    h = h_ref[...]
    acc = None
    for y_ref, wg, bg, wbr in ((ya_ref, wg0, bg0, wa_ref), (yb_ref, wg1, bg1, wb_ref),
                               (yc_ref, wg2, bg2, wc_ref), (ym_ref, wg3, bg3, wm_ref)):
        gate = jax.nn.sigmoid(_dot(h, wg[...].astype(BF16)) + bg[...])
        term = gate * _dot(y_ref[...], wbr[...].astype(BF16))
        acc = term if acc is None else acc + term
    o_ref[...] = acc.astype(o_ref.dtype)


def _gate_block(i, j, *, layer, branch, n_tiles):
    return (layer, 0, branch * n_tiles + j)


def gated_merge(h, ys, w_gate, b_gate, w_brs, layers, tm=1024, tn=256):
    m, d = h.shape
    n_tiles = d // tn
    row = lambda width: pl.BlockSpec((tm, width), lambda i, j: (i, 0))
    gate_maps = [functools.partial(_gate_block, layer=layers[0], branch=br, n_tiles=n_tiles) for br in range(4)]
    gate_w = [pl.BlockSpec((None, d, tn), gm) for gm in gate_maps]
    gate_b = [pl.BlockSpec((None, 1, tn), gm) for gm in gate_maps]
    br_w = [pl.BlockSpec((None, w.shape[1], tn), functools.partial(lambda i, j, l: (l, 0, j), l=l))
            for w, l in zip(w_brs, layers[1:])]
    b_gate3 = b_gate.reshape(b_gate.shape[0], 1, -1)
    return pl.pallas_call(
        _merge_kernel,
        grid=(m // tm, n_tiles),
        in_specs=[row(d)] + [row(y.shape[1]) for y in ys] + gate_w + gate_b + br_w,
        out_specs=pl.BlockSpec((tm, tn), lambda i, j: (i, j)),
        out_shape=jax.ShapeDtypeStruct((m, d), BF16),
        compiler_params=_params(2),
        name="gated_merge",
    )(h, *ys, w_gate, w_gate, w_gate, w_gate, b_gate3, b_gate3, b_gate3, b_gate3, *w_brs)


def _silu_mul(g, u):
    return g * jax.nn.sigmoid(g) * u


def _swiglu_up_kernel(x_ref, g_ref, wg_ref, wu_ref, o_ref, h_scr):
    @pl.when(pl.program_id(1) == 0)
    def _():
        h_scr[...] = _rmsnorm_body(x_ref[...], g_ref[...]).astype(h_scr.dtype)

    a = h_scr[...]
    o_ref[...] = _silu_mul(_dot(a, wg_ref[...].astype(BF16)), _dot(a, wu_ref[...].astype(BF16))).astype(o_ref.dtype)


def swiglu_up(x, norm_g, wg, wu, layer, tm=1024, tf=512):
    m, k = x.shape
    f = wg.shape[2]
    return pl.pallas_call(
        _swiglu_up_kernel,
        grid=(m // tm, f // tf),
        in_specs=[pl.BlockSpec((tm, k), lambda i, j: (i, 0)),
                  pl.BlockSpec((1, k), lambda i, j: (0, 0)),
                  pl.BlockSpec((None, k, tf), lambda i, j: (layer, 0, j)),
                  pl.BlockSpec((None, k, tf), lambda i, j: (layer, 0, j))],
        out_specs=pl.BlockSpec((tm, tf), lambda i, j: (i, j)),
        out_shape=jax.ShapeDtypeStruct((m, f), BF16),
        scratch_shapes=[pltpu.VMEM((tm, k), BF16)],
        compiler_params=_params(2),
        name="swiglu_up",
    )(x, norm_g.reshape(1, k), wg, wu)


def _router_kernel(x_ref, g_ref, r_ref, idx_ref, rank_ref, gate_ref, cnt_ref):
    @pl.when(pl.program_id(0) == 0)
    def _():
        cnt_ref[...] = jnp.zeros_like(cnt_ref)

    h = _rmsnorm_body(x_ref[...], g_ref[...])
    r = r_ref[...]
    h_hi, r_hi = h.astype(BF16), r.astype(BF16)
    h_lo, r_lo = (h - h_hi.astype(F32)).astype(BF16), (r - r_hi.astype(F32)).astype(BF16)
    logits = _dot(h_hi, r_hi) + (_dot(h_hi, r_lo) + _dot(h_lo, r_hi))
    lane = lax.broadcasted_iota(jnp.int32, logits.shape, 1).astype(F32)
    neg = -jnp.inf
    lg = jnp.where(lane < N_EXPERTS, logits, neg)
    v1 = jnp.max(lg, axis=-1, keepdims=True)
    i1 = jnp.min(jnp.where(lg == v1, lane, float(LANES)), axis=-1, keepdims=True)
    lg2 = jnp.where(lane == i1, neg, lg)
    v2 = jnp.max(lg2, axis=-1, keepdims=True)
    i2 = jnp.min(jnp.where(lg2 == v2, lane, float(LANES)), axis=-1, keepdims=True)
    e = jnp.exp(v2 - v1)
    g1 = 1.0 / (1.0 + e)
    g2 = e / (1.0 + e)
    onehot = jnp.where((lane == i1) | (lane == i2), 1.0, 0.0)
    tt = onehot.shape[0]
    earlier = lax.broadcasted_iota(jnp.int32, (tt, tt), 1) < lax.broadcasted_iota(jnp.int32, (tt, tt), 0)
    pos = _dot(jnp.where(earlier, 1.0, 0.0).astype(BF16), onehot.astype(BF16)) + cnt_ref[...]
    r1 = jnp.sum(jnp.where(lane == i1, pos, 0.0), axis=-1, keepdims=True)
    r2 = jnp.sum(jnp.where(lane == i2, pos, 0.0), axis=-1, keepdims=True)
    cnt_ref[...] += jnp.sum(onehot, axis=0, keepdims=True)
    idx_ref[...] = jnp.where(lane == 0, i1, jnp.where(lane == 1, i2, 0.0)).astype(jnp.int32)
    rank_ref[...] = jnp.where(lane == 0, r1, jnp.where(lane == 1, r2, 0.0)).astype(jnp.int32)
    gate_ref[...] = jnp.where(lane == 0, g1, jnp.where(lane == 1, g2, 0.0))


def route(x, g, router, tm=512):
    m, d = x.shape
    r_pad = jnp.pad(router, ((0, 0), (0, LANES - N_EXPERTS)))
    lane_spec = pl.BlockSpec((tm, LANES), lambda i: (i, 0))
    return pl.pallas_call(
        _router_kernel,
        grid=(m // tm,),
        in_specs=[pl.BlockSpec((tm, d), lambda i: (i, 0)),
                  pl.BlockSpec((1, d), lambda i: (0, 0)),
                  pl.BlockSpec((d, LANES), lambda i: (0, 0))],
        out_specs=[lane_spec, lane_spec, lane_spec, pl.BlockSpec((1, LANES), lambda i: (0, 0))],
        out_shape=[jax.ShapeDtypeStruct((m, LANES), jnp.int32),
                   jax.ShapeDtypeStruct((m, LANES), jnp.int32),
                   jax.ShapeDtypeStruct((m, LANES), F32),
                   jax.ShapeDtypeStruct((1, LANES), F32)],
        compiler_params=_params(1),
        name="route",
    )(x, g.reshape(1, d), r_pad)


def _dispatch_kernel(dest_ref, lo_ref, hi_ref, na_ref, x_hbm, g_ref, a_ref, src_ref, rows_ref, sems):
    j = pl.program_id(0)
    tm = rows_ref.shape[1]
    n_active = na_ref[0]

    def row_copy(tile, r):
        slot = lax.rem(tile, 2)
        return pltpu.make_async_copy(x_hbm.at[pl.ds(src_ref[tile * tm + r], 1)],
                                     rows_ref.at[slot, pl.ds(r, 1)], sems.at[slot])

    def start_tile(tile):
        def start(r, c):
            row_copy(tile, r).start()
            return c

        lax.fori_loop(0, tm, start, 0, unroll=8)

    @pl.when(j == 0)
    def _():
        def clear(i, c):
            src_ref[i] = 0
            return c

        for e in range(N_EXPERTS):
            lax.fori_loop(lo_ref[e], hi_ref[e], clear, 0)

        def fill(i, c):
            src_ref[dest_ref[i]] = lax.shift_right_logical(i, 1)
            return c

        lax.fori_loop(0, dest_ref.shape[0], fill, 0, unroll=8)

        @pl.when(n_active > 0)
        def _():
            start_tile(0)

    @pl.when(j + 1 < n_active)
    def _():
        start_tile(j + 1)

    @pl.when(j < n_active)
    def _():
        def wait(r, c):
            row_copy(j, r).wait()
            return c

        lax.fori_loop(0, tm, wait, 0, unroll=8)
        a_ref[...] = _rmsnorm_body(rows_ref[lax.rem(j, 2)], g_ref[...]).astype(a_ref.dtype)

    @pl.when(j >= n_active)
    def _():
        a_ref[...] = jnp.zeros_like(a_ref)


def _combine_kernel(dest_ref, x_ref, gate_ref, g_ref, y_hbm, o_ref, rows_ref, sems, *, apply_norm):
    tt = x_ref.shape[0]
    i = pl.program_id(0)

    def row_copy(tile, r, k):
        slot = lax.rem(tile, 2)
        row = dest_ref[TOP_K * (tile * tt + r) + k]
        return pltpu.make_async_copy(y_hbm.at[pl.ds(row, 1)], rows_ref.at[slot, k, pl.ds(r, 1)], sems.at[slot])

    def start_tile(tile):
        def start(r, c):
            for k in range(TOP_K):
                row_copy(tile, r, k).start()
            return c

        lax.fori_loop(0, tt, start, 0, unroll=4)

    @pl.when(i == 0)
    def _():
        start_tile(0)

    @pl.when(i + 1 < pl.num_programs(0))
    def _():
        start_tile(i + 1)

    def wait(r, c):
        for k in range(TOP_K):
            row_copy(i, r, k).wait()
        return c

    lax.fori_loop(0, tt, wait, 0, unroll=4)
    gates = gate_ref[...]
    slot = lax.rem(i, 2)
    y = x_ref[...]
    for k in range(TOP_K):
        y = y + gates[:, k:k + 1] * rows_ref[slot, k]
    o_ref[...] = _rmsnorm_body(y, g_ref[...]) if apply_norm else y


def _moe_up_kernel(te_ref, na_ref, a_ref, wg_ref, wu_ref, o_ref):
    @pl.when(pl.program_id(1) < na_ref[0])
    def _():
        a = a_ref[...]
        o_ref[...] = _silu_mul(_dot(a, wg_ref[...].astype(BF16)), _dot(a, wu_ref[...].astype(BF16))).astype(o_ref.dtype)

    @pl.when(pl.program_id(1) >= na_ref[0])
    def _():
        o_ref[...] = jnp.zeros_like(o_ref)


def _moe_down_kernel(te_ref, na_ref, a_ref, w_ref, o_ref, *, tk):
    @pl.when(pl.program_id(1) < na_ref[0])
    def _():
        acc = None
        for c in range(a_ref.shape[1] // tk):
            part = _dot(a_ref[:, c * tk:(c + 1) * tk], w_ref[c * tk:(c + 1) * tk, :].astype(BF16))
            acc = part if acc is None else acc + part
        o_ref[...] = acc

    @pl.when(pl.program_id(1) >= na_ref[0])
    def _():
        o_ref[...] = jnp.zeros_like(o_ref)


def moe_layer(x, norm_g, router, wg, wu, wd, layer, out_g, tm=512, tf=1024, tn=512, tk=1024, tt=256):
    assert TOP_K == 2
    t, d = x.shape
    _, n_exp, _, f = wg.shape
    n_rows = t * TOP_K
    n_tiles = n_rows // tm + n_exp
    idx, rank, gate, cnt = route(x, norm_g, router)
    counts = cnt[0, :n_exp].astype(jnp.int32)
    tiles_per = (counts + tm - 1) // tm
    tile_end = jnp.cumsum(tiles_per)
    start = (tile_end - tiles_per) * tm
    n_active = tile_end[-1:]
    tile_ids = jnp.arange(n_tiles, dtype=jnp.int32)
    tile_expert = jnp.sum((tile_ids[:, None] >= tile_end[None, :]).astype(jnp.int32), axis=1)
    last_expert = jnp.sum((n_active - 1 >= tile_end).astype(jnp.int32))
    tile_expert = jnp.where(tile_ids < n_active, tile_expert, last_expert).astype(jnp.int32)
    picked = idx[:, :TOP_K, None] == jnp.arange(n_exp, dtype=jnp.int32)
    dest = (jnp.sum(jnp.where(picked, start, 0), axis=-1) + rank[:, :TOP_K]).reshape(n_rows)

    row_tile = lambda j, na: jnp.maximum(jnp.minimum(j, na[0] - 1), 0)
    a = pl.pallas_call(
        _dispatch_kernel,
        grid_spec=pltpu.PrefetchScalarGridSpec(
            num_scalar_prefetch=4,
            grid=(n_tiles,),
            in_specs=[pl.BlockSpec(memory_space=pl.ANY),
                      pl.BlockSpec((1, d), lambda j, dst, lo, hi, na: (0, 0))],
            out_specs=pl.BlockSpec((tm, d), lambda j, dst, lo, hi, na: (j, 0)),
            scratch_shapes=[pltpu.SMEM((n_tiles * tm,), jnp.int32),
                            pltpu.VMEM((2, tm, d), F32),
                            pltpu.SemaphoreType.DMA((2,))]),
        out_shape=jax.ShapeDtypeStruct((n_tiles * tm, d), BF16),
        compiler_params=_params(1),
        name="moe_dispatch",
    )(dest, start + counts, tile_end * tm, n_active, x, norm_g.reshape(1, d))
    up = pl.pallas_call(
        _moe_up_kernel,
        grid_spec=pltpu.PrefetchScalarGridSpec(
            num_scalar_prefetch=2,
            grid=(f // tf, n_tiles),
            in_specs=[pl.BlockSpec((tm, d), lambda c, j, te, na: (row_tile(j, na), 0)),
                      pl.BlockSpec((None, None, d, tf), lambda c, j, te, na: (layer, te[j], 0, c)),
                      pl.BlockSpec((None, None, d, tf), lambda c, j, te, na: (layer, te[j], 0, c))],
            out_specs=pl.BlockSpec((tm, tf), lambda c, j, te, na: (j, c))),
        out_shape=jax.ShapeDtypeStruct((n_tiles * tm, f), BF16),
        compiler_params=_params(2),
        name="moe_up",
    )(tile_expert, n_active, a, wg, wu)
    down = pl.pallas_call(
        functools.partial(_moe_down_kernel, tk=tk),
        grid_spec=pltpu.PrefetchScalarGridSpec(
            num_scalar_prefetch=2,
            grid=(d // tn, n_tiles),
            in_specs=[pl.BlockSpec((tm, f), lambda c, j, te, na: (row_tile(j, na), 0)),
                      pl.BlockSpec((None, None, f, tn), lambda c, j, te, na: (layer, te[j], 0, c))],
            out_specs=pl.BlockSpec((tm, tn), lambda c, j, te, na: (j, c))),
        out_shape=jax.ShapeDtypeStruct((n_tiles * tm, d), F32),
        compiler_params=_params(2),
        name="moe_down",
    )(tile_expert, n_active, up, wd)
    g_out = jnp.ones((d,), F32) if out_g is None else out_g
    return pl.pallas_call(
        functools.partial(_combine_kernel, apply_norm=out_g is not None),
        grid_spec=pltpu.PrefetchScalarGridSpec(
            num_scalar_prefetch=1,
            grid=(t // tt,),
            in_specs=[pl.BlockSpec((tt, d), lambda i, dst: (i, 0)),
                      pl.BlockSpec((tt, LANES), lambda i, dst: (i, 0)),
                      pl.BlockSpec((1, d), lambda i, dst: (0, 0)),
                      pl.BlockSpec(memory_space=pl.ANY)],
            out_specs=pl.BlockSpec((tt, d), lambda i, dst: (i, 0)),
            scratch_shapes=[pltpu.VMEM((2, TOP_K, tt, d), F32),
                            pltpu.SemaphoreType.DMA((2,))]),
        out_shape=jax.ShapeDtypeStruct((t, d), F32),
        compiler_params=_params(1),
        name="moe_combine",
    )(dest, x, gate, g_out.reshape(1, d), down)


def _mixer(x, mem_n, b, s, layer, w_in, rpb, w_mem_kv, w_br_a, w_br_b, w_br_c, w_br_m, w_gate, b_gate, w_out,
           norm_g):
    t = b * s
    h = rmsnorm(x, norm_g, BF16)
    z = matmul(h, w_in, layer, tm=2048, tn=NA_W, name="in_proj").reshape(b, s, -1)
    kv_m = matmul(mem_n, w_mem_kv, layer, tm=mem_n.shape[0], tn=MEM_W, name="mem_kv_proj")
    y_a = neighbourhood_attention(z, rpb).reshape(t, NA_W)
    y_b = fourier_mix(z).reshape(t, FN_W)
    y_c = dilated_attention(z)
    y_m = memory_attention(z, kv_m.reshape(b, -1, 2 * MEM_W)).reshape(t, MEM_W)
    merged = gated_merge(h, (y_a, y_b, y_c, y_m), w_gate, b_gate, (w_br_a, w_br_b, w_br_c, w_br_m),
                         (layer,) * 5)
    return matmul(merged, w_out, layer, tm=2048, tn=512, out_dtype=F32, residual=x, name="out_proj")


def kernel(x, mem, norm_mix, w_in, rpb, norm_mem, w_mem_kv, w_br_a, w_br_b, w_br_c, w_br_m, w_gate, b_gate,
           w_out, norm_ffn, dense_w_gate, dense_w_up, dense_w_down, router, exp_w_gate, exp_w_up, exp_w_down,
           final_norm):
    b, s, d = x.shape
    depth = norm_mix.shape[0]
    t = b * s
    xf = x.reshape(t, d)
    memf = mem.reshape(-1, d)
    normed = False
    for layer in range(depth):
        mem_n = rmsnorm(memf, norm_mem[layer], BF16)
        xf = _mixer(xf, mem_n, b, s, layer, w_in, rpb[layer], w_mem_kv, w_br_a, w_br_b, w_br_c, w_br_m, w_gate,
                    b_gate, w_out, norm_mix[layer])
        i = layer // 2
        if layer % 2 == 0:
            u = swiglu_up(xf, norm_ffn[layer], dense_w_gate, dense_w_up, i)
            xf = matmul(u, dense_w_down, i, tm=1024, tn=256, out_dtype=F32, residual=xf, name="swiglu_down")
        else:
            normed = layer == depth - 1
            xf = moe_layer(xf, norm_ffn[layer], router[i], exp_w_gate, exp_w_up, exp_w_down, i,
                           final_norm if normed else None)
    if not normed:
        xf = rmsnorm(xf, final_norm, F32)
    return xf.reshape(b, s, d)
```

```python
import functools

import numpy as np
import jax
import jax.numpy as jnp
from jax import lax
from jax.experimental import pallas as pl
from jax.experimental.pallas import tpu as pltpu

F32 = jnp.float32
BF16 = jnp.bfloat16

D_MODEL = 2048
GRID_W = 64
HEAD_DIM = 128
RMS_EPS = 1e-6
NA_HEADS = 6
NA_KH = 8
NA_KW = 16
NA_W = NA_HEADS * HEAD_DIM
FN_GROUPS = 6
FN_GROUP_W = 128
FN_W = FN_GROUPS * FN_GROUP_W
DIL_DILATIONS = (1, 4, 16)
DIL_HALF_WINDOW = 64
DIL_HEADS_PER_GROUP = 2
DIL_GROUP_W = DIL_HEADS_PER_GROUP * HEAD_DIM
Q_BLOCK = 128
ROT_DIM = HEAD_DIM // 4
ROPE_THETA = 500000.0
MEM_HEADS = 4
MEM_HEAD_DIM = 192
MEM_W = MEM_HEADS * MEM_HEAD_DIM
MEM_WINDOW = 256
N_EXPERTS = 8
TOP_K = 2
MASK_VALUE = -1e30

V7X_VMEM_BYTES = 64 * 1024 * 1024
VMEM_LIMIT_BYTES = V7X_VMEM_BYTES - 4 * 1024 * 1024
LANES = 128
N_DMA_PRIORITIES = 2


def _params(n_axes):
    return pltpu.CompilerParams(dimension_semantics=("arbitrary",) * n_axes,
                                vmem_limit_bytes=VMEM_LIMIT_BYTES)


def _dot(a, b):
    return jnp.dot(a, b, preferred_element_type=F32)


def _dot_nt(a, b):
    return lax.dot_general(a, b, (((1,), (1,)), ((), ())), preferred_element_type=F32)


def _softmax_pv(s, v):
    d = v.shape[1]
    top = jnp.max(s, axis=-1, keepdims=True)
    p = jnp.exp((s - top).astype(BF16))
    pv = _dot(p, jnp.concatenate([v, jnp.ones_like(v)], axis=1))
    num, den = pv[:, :d], pv[:, d:]
    return num / den, top + jnp.log(den)


def _rmsnorm_body(x, g):
    return x * lax.rsqrt(jnp.mean(x * x, axis=-1, keepdims=True) + RMS_EPS) * g


def _rmsnorm_kernel(x_ref, g_ref, o_ref):
    o_ref[...] = _rmsnorm_body(x_ref[...], g_ref[...]).astype(o_ref.dtype)


def rmsnorm(x, g, out_dtype, tm=512):
    m, d = x.shape
    return pl.pallas_call(
        _rmsnorm_kernel,
        grid=(m // tm,),
        in_specs=[pl.BlockSpec((tm, d), lambda i: (i, 0)),
                  pl.BlockSpec((1, d), lambda i: (0, 0))],
        out_specs=pl.BlockSpec((tm, d), lambda i: (i, 0)),
        out_shape=jax.ShapeDtypeStruct((m, d), out_dtype),
        compiler_params=_params(1),
        name="rmsnorm",
    )(x, g.reshape(1, d))


def _mm_kernel(a_ref, w_ref, o_ref):
    o_ref[...] = _dot(a_ref[...], w_ref[...].astype(BF16)).astype(o_ref.dtype)


def _mm_res_kernel(a_ref, w_ref, r_ref, o_ref):
    o_ref[...] = (r_ref[...] + _dot(a_ref[...], w_ref[...].astype(BF16))).astype(o_ref.dtype)


def matmul(a, w, layer, *, tm, tn, n_cols=None, out_dtype=BF16, residual=None, name="matmul"):
    m, k = a.shape
    n = w.shape[2] if n_cols is None else n_cols
    in_specs = [pl.BlockSpec((tm, k), lambda i, j: (i, 0)),
                pl.BlockSpec((None, k, tn), lambda i, j: (layer, 0, j))]
    args = [a, w]
    kernel = _mm_kernel
    if residual is not None:
        in_specs.append(pl.BlockSpec((tm, tn), lambda i, j: (i, j)))
        args.append(residual)
        kernel = _mm_res_kernel
    return pl.pallas_call(
        kernel,
        grid=(m // tm, n // tn),
        in_specs=in_specs,
        out_specs=pl.BlockSpec((tm, tn), lambda i, j: (i, j)),
        out_shape=jax.ShapeDtypeStruct((m, n), out_dtype),
        compiler_params=_params(2),
        name=name,
    )(*args)


NA_Q_ROWS = 4
NA_KEY_ROWS = NA_KH + NA_Q_ROWS


def _na_key_row_start(step, rows):
    return np.clip(step * NA_Q_ROWS - NA_KH // 2, 0, rows - NA_KEY_ROWS)


def _na_bias_table(rpb, rows):
    col = np.arange(GRID_W)
    col_start = np.clip(col - NA_KW // 2, 0, GRID_W - NA_KW)
    rel_col = col[None, :] - col[:, None] + (NA_KW - 1)
    col_ok = (col[None, :] >= col_start[:, None]) & (col[None, :] < col_start[:, None] + NA_KW)
    pick = (np.clip(rel_col, 0, 2 * NA_KW - 2)[None] == np.arange(2 * NA_KW - 1)[:, None, None]).astype(np.float32)
    slab = jnp.einsum("hdr,rcw->hdcw", rpb.astype(F32), pick, precision=lax.Precision.HIGHEST)
    n_steps = rows // NA_Q_ROWS
    rel_row = np.zeros((n_steps, NA_Q_ROWS, NA_KEY_ROWS), np.int64)
    row_ok = np.zeros((n_steps, NA_Q_ROWS, NA_KEY_ROWS), bool)
    for step in range(n_steps):
        for i in range(NA_Q_ROWS):
            r = step * NA_Q_ROWS + i
            win = np.clip(r - NA_KH // 2, 0, rows - NA_KH)
            key_row = _na_key_row_start(step, rows) + np.arange(NA_KEY_ROWS)
            row_ok[step, i] = (key_row >= win) & (key_row < win + NA_KH)
            rel_row[step, i] = key_row - r + (NA_KH - 1)
    assert all((rel_row[s_] * row_ok[s_] == rel_row[1] * row_ok[1]).all() and (row_ok[s_] == row_ok[1]).all()
               for s_ in range(1, n_steps - 1))
    slab = jnp.where(col_ok, slab, MASK_VALUE)
    hidden = jnp.full(slab.shape[:1] + slab.shape[2:], MASK_VALUE, F32)
    cases = []
    for step in (0, 1, n_steps - 1):
        q_rows = [jnp.concatenate([slab[:, rel_row[step, i, k]] if row_ok[step, i, k] else hidden
                                   for k in range(NA_KEY_ROWS)], axis=-1) for i in range(NA_Q_ROWS)]
        cases.append(jnp.concatenate(q_rows, axis=-2))
    return jnp.stack(cases)


def _na_kernel(q_ref, k_ref, v_ref, b_ref, o_ref, *, rows):
    step = pl.program_id(1)
    key_row0 = jnp.clip(step * NA_Q_ROWS - NA_KH // 2, 0, rows - NA_KEY_ROWS)
    k0 = pl.multiple_of(key_row0 * GRID_W, GRID_W)
    n_keys = NA_KEY_ROWS * GRID_W
    scale = HEAD_DIM ** -0.5
    for h in range(NA_HEADS):
        cols = slice(h * HEAD_DIM, (h + 1) * HEAD_DIM)
        k = k_ref[0, pl.ds(k0, n_keys), cols]
        v = v_ref[0, pl.ds(k0, n_keys), cols]
        s = _dot_nt(q_ref[0, :, cols], k) * scale + b_ref[0, h]
        o_ref[0, :, cols] = _softmax_pv(s, v)[0].astype(o_ref.dtype)


def neighbourhood_attention(z, rpb):
    b, s, _ = z.shape
    rows = s // GRID_W
    n_steps = rows // NA_Q_ROWS
    bias = _na_bias_table(rpb, rows)
    tq = NA_Q_ROWS * GRID_W
    bias_case = lambda i, r: (jnp.where(r == 0, 0, jnp.where(r == n_steps - 1, 2, 1)), 0, 0, 0)
    return pl.pallas_call(
        functools.partial(_na_kernel, rows=rows),
        grid=(b, n_steps),
        in_specs=[pl.BlockSpec((1, tq, NA_W), lambda i, r: (i, r, 0)),
                  pl.BlockSpec((1, s, NA_W), lambda i, r: (i, 0, 1)),
                  pl.BlockSpec((1, s, NA_W), lambda i, r: (i, 0, 2)),
                  pl.BlockSpec((1,) + bias.shape[1:], bias_case)],
        out_specs=pl.BlockSpec((1, tq, NA_W), lambda i, r: (i, r, 0)),
        out_shape=jax.ShapeDtypeStruct((b, s, NA_W), BF16),
        compiler_params=_params(2),
        name="neighbourhood_attention",
    )(z, z, z, bias)


def _dft_tables(s):
    half = s // 2
    n = np.arange(half, dtype=np.int64)
    ang = 2.0 * np.pi * ((n[:, None] * n[None, :]) % half) / half
    pos = np.concatenate([np.cos(ang), -np.sin(ang)], axis=1)
    c = np.arange(FN_GROUP_W, dtype=np.int64)
    ang_c = 2.0 * np.pi * ((c[:, None] * c[None, :]) % FN_GROUP_W) / FN_GROUP_W
    chan = np.concatenate([np.cos(ang_c), np.sin(ang_c)], axis=1)
    tw = np.pi * n[:, None] / half * np.ones((1, FN_GROUP_W))
    return (jnp.asarray(pos, F32).astype(BF16), jnp.asarray(chan, F32).astype(BF16),
            jnp.asarray(np.cos(tw), F32), jnp.asarray(np.sin(tw), F32))


def _fourier_kernel(u_ref, chan_ref, twc_ref, tws_ref, pos_ref, o_ref, ve_scr, vo_scr, y_scr, *, s, norm):
    half = s // 2

    @pl.when(pl.program_id(1) == 0)
    def _():
        twc, tws = twc_ref[...], tws_ref[...]
        for g in range(FN_GROUPS):
            cols = slice(g * FN_GROUP_W, (g + 1) * FN_GROUP_W)
            t = _dot(u_ref[0, :, cols], chan_ref[...])
            top, bot = t[:half], t[half:]
            fold = top + bot
            diff = top - bot
            dc, ds = diff[:, :FN_GROUP_W], diff[:, FN_GROUP_W:]
            ve_scr[0:half, cols] = fold[:, :FN_GROUP_W].astype(BF16)
            ve_scr[half:s, cols] = fold[:, FN_GROUP_W:].astype(BF16)
            vo_scr[0:half, cols] = (twc * dc - tws * ds).astype(BF16)
            vo_scr[half:s, cols] = (tws * dc + twc * ds).astype(BF16)

    rows = pos_ref.shape[0]
    y_even = _dot(pos_ref[...], ve_scr[...]) * norm
    y_odd = _dot(pos_ref[...], vo_scr[...]) * norm
    for g in range(FN_GROUPS):
        cols = slice(g * FN_GROUP_W, (g + 1) * FN_GROUP_W)
        y_scr[g, pl.ds(0, rows, stride=2), :] = y_even[:, cols]
        y_scr[g, pl.ds(1, rows, stride=2), :] = y_odd[:, cols]
        o_ref[0, :, cols] = y_scr[g].astype(o_ref.dtype)


def fourier_mix(z, ts=512):
    b, s, _ = z.shape
    pos_tab, chan_tab, tw_cos, tw_sin = _dft_tables(s)
    norm = float((s * FN_GROUP_W) ** -0.5)
    whole = lambda a: pl.BlockSpec(a.shape, lambda i, j: (0, 0))
    return pl.pallas_call(
        functools.partial(_fourier_kernel, s=s, norm=norm),
        grid=(b, s // (2 * ts)),
        in_specs=[pl.BlockSpec((1, s, FN_W), lambda i, j: (i, 0, 6)),
                  whole(chan_tab), whole(tw_cos), whole(tw_sin),
                  pl.BlockSpec((ts, s), lambda i, j: (j, 0))],
        out_specs=pl.BlockSpec((1, 2 * ts, FN_W), lambda i, j: (i, j, 0)),
        out_shape=jax.ShapeDtypeStruct((b, s, FN_W), BF16),
        scratch_shapes=[pltpu.VMEM((s, FN_W), BF16), pltpu.VMEM((s, FN_W), BF16),
                        pltpu.VMEM((FN_GROUPS, 2 * ts, FN_GROUP_W), F32)],
        compiler_params=_params(2),
        name="fourier_mix",
    )(z, chan_tab, tw_cos, tw_sin, pos_tab)


def _rope_tables(s):
    half = ROT_DIM // 2
    inv_freq = ROPE_THETA ** (-2.0 * jnp.arange(half, dtype=F32) / ROT_DIM)
    ang = jnp.arange(s, dtype=jnp.int32).astype(F32)[:, None] * inv_freq[None, :]
    cos, sin = jnp.cos(ang), jnp.sin(ang)
    ones = jnp.ones((s, HEAD_DIM - ROT_DIM), F32)
    zeros_half = jnp.zeros((s, half), F32)
    zeros_rest = jnp.zeros((s, HEAD_DIM - ROT_DIM), F32)
    cos_t = jnp.concatenate([cos, cos, ones], axis=1)
    sin_lo = jnp.concatenate([-sin, zeros_half, zeros_rest], axis=1)
    sin_hi = jnp.concatenate([zeros_half, sin, zeros_rest], axis=1)
    return cos_t, sin_lo, sin_hi


def _dilated_kernel(q0, q1, q2, k0, k1, k2, v0, v1, v2, cos_ref, slo_ref, shi_ref, y_ref,
                    rq_ref, rk_ref, rv_ref, o_scr, l_scr, *, seq):
    half = ROT_DIM // 2
    scale = HEAD_DIM ** -0.5
    chunk = 4 * Q_BLOCK
    for g, (q_ref, k_ref, v_ref) in enumerate(((q0, k0, v0), (q1, k1, v1), (q2, k2, v2))):
        for c0 in range(0, seq, chunk):
            rows = slice(c0, c0 + chunk)
            cs, lo, hi = cos_ref[rows, :], slo_ref[rows, :], shi_ref[rows, :]
            for src, dst in ((q_ref, rq_ref), (k_ref, rk_ref)):
                x = src[0, rows, :].astype(F32)
                dst[g, rows, :] = x * cs + pltpu.roll(x, HEAD_DIM - half, 1) * lo + pltpu.roll(x, half, 1) * hi
            rv_ref[g, rows, :] = v_ref[0, rows, :].astype(F32)

    masks = {}

    def window_mask(n_keys, offset):
        if (n_keys, offset) not in masks:
            rel = (lax.broadcasted_iota(jnp.int32, (Q_BLOCK, n_keys), 1)
                   - lax.broadcasted_iota(jnp.int32, (Q_BLOCK, n_keys), 0) + offset)
            masks[(n_keys, offset)] = jnp.abs(rel) <= DIL_HALF_WINDOW
        return masks[(n_keys, offset)]

    for g, d in enumerate(DIL_DILATIONS):
        length = seq // d
        n_keys = min(length, Q_BLOCK + 2 * DIL_HALF_WINDOW)
        for rho in range(d):
            for m_q in range(0, length, Q_BLOCK):
                m_k = min(max(m_q - DIL_HALF_WINDOW, 0), length - n_keys)
                q_rows = pl.ds(rho + d * m_q, Q_BLOCK, stride=d) if d > 1 else pl.ds(m_q, Q_BLOCK)
                k_rows = pl.ds(rho + d * m_k, n_keys, stride=d) if d > 1 else pl.ds(m_k, n_keys)
                q = rq_ref[g, q_rows, :].astype(BF16)
                k = rk_ref[g, k_rows, :].astype(BF16)
                v = rv_ref[g, k_rows, :].astype(BF16)
                s = jnp.where(window_mask(n_keys, m_k - m_q), _dot_nt(q, k) * scale, MASK_VALUE)
                o_scr[g, q_rows, :], l_scr[g, q_rows, :] = _softmax_pv(s, v)

    n_groups = len(DIL_DILATIONS)
    for c0 in range(0, seq, chunk):
        rows = slice(c0, c0 + chunk)
        lses = [l_scr[g, rows, :] for g in range(n_groups)]
        top = functools.reduce(jnp.maximum, lses)
        ws = [jnp.exp(l - top) for l in lses]
        num = sum(w * o_scr[g, rows, :] for g, w in enumerate(ws))
        y_ref[0, rows, :] = (num / sum(ws)).astype(y_ref.dtype)


def dilated_attention(z):
    b, s, _ = z.shape
    n_groups = len(DIL_DILATIONS)
    tables = _rope_tables(s)
    heads_per_block = NA_W // HEAD_DIM

    def head_spec(t, g):
        first = (3 + t) * heads_per_block + g * DIL_HEADS_PER_GROUP
        return pl.BlockSpec((1, s, HEAD_DIM), lambda i, h: (i, 0, first + h))

    table_spec = pl.BlockSpec((s, HEAD_DIM), lambda i, h: (0, 0))
    scratch = pltpu.VMEM((n_groups, s, HEAD_DIM), F32)
    return pl.pallas_call(
        functools.partial(_dilated_kernel, seq=s),
        grid=(b, DIL_HEADS_PER_GROUP),
        in_specs=[head_spec(t, g) for t in range(3) for g in range(n_groups)] + [table_spec] * 3,
        out_specs=pl.BlockSpec((1, s, HEAD_DIM), lambda i, h: (i, 0, h)),
        out_shape=jax.ShapeDtypeStruct((b, s, DIL_GROUP_W), BF16),
        scratch_shapes=[scratch] * 5,
        compiler_params=_params(2),
        name="dilated_attention",
    )(*([z] * 9), *tables).reshape(b * s, DIL_GROUP_W)


def _mem_attn_kernel(q_ref, k_ref, v_ref, o_ref):
    scale = MEM_HEAD_DIM ** -0.5
    col = lax.broadcasted_iota(jnp.int32, (1, MEM_WINDOW), 1)
    parts = []
    for h in range(MEM_HEADS):
        start = h * MEM_HEAD_DIM // LANES * LANES
        win = slice(start, start + MEM_WINDOW)
        in_head = (col >= h * MEM_HEAD_DIM - start) & (col < (h + 1) * MEM_HEAD_DIM - start)
        q = q_ref[0, :, win]
        s = _dot_nt(jnp.where(in_head, q, jnp.zeros_like(q)), k_ref[0, :, win]) * scale
        o = jnp.where(in_head, _softmax_pv(s, v_ref[0, :, win])[0], 0.0)
        pads = (start, MEM_W - start - MEM_WINDOW)
        left, right = (jnp.zeros((o.shape[0], n), F32) for n in pads)
        parts.append(jnp.concatenate([a for a in (left, o, right) if a.shape[1]], axis=1))
    o_ref[0] = sum(parts).astype(o_ref.dtype)


def memory_attention(z, kv, tq=512):
    b, s, _ = z.shape
    m = kv.shape[1]
    return pl.pallas_call(
        _mem_attn_kernel,
        grid=(b, s // tq),
        in_specs=[pl.BlockSpec((1, tq, MEM_W), lambda i, j: (i, j, 7)),
                  pl.BlockSpec((1, m, MEM_W), lambda i, j: (i, 0, 0)),
                  pl.BlockSpec((1, m, MEM_W), lambda i, j: (i, 0, 1))],
        out_specs=pl.BlockSpec((1, tq, MEM_W), lambda i, j: (i, j, 0)),
        out_shape=jax.ShapeDtypeStruct((b, s, MEM_W), BF16),
        compiler_params=_params(2),
        name="memory_attention",
    )(z, kv, kv)


def _merge_kernel(h_ref, ya_ref, yb_ref, yc_ref, ym_ref, wg0, wg1, wg2, wg3, bg0, bg1, bg2, bg3,
                  wa_ref, wb_ref, wc_ref, wm_ref, o_ref):
    h = h_ref[...]
    acc = None
    for y_ref, wg, bg, wbr in ((ya_ref, wg0, bg0, wa_ref), (yb_ref, wg1, bg1, wb_ref),
                               (yc_ref, wg2, bg2, wc_ref), (ym_ref, wg3, bg3, wm_ref)):
        gate = jax.nn.sigmoid(_dot(h, wg[...].astype(BF16)) + bg[...])
        term = gate * _dot(y_ref[...], wbr[...].astype(BF16))
        acc = term if acc is None else acc + term
    o_ref[...] = acc.astype(o_ref.dtype)


def _gate_block(i, j, *, layer, branch, n_tiles):
    return (layer, 0, branch * n_tiles + j)


def gated_merge(h, ys, w_gate, b_gate, w_brs, layers, tm=1024, tn=256):
    m, d = h.shape
    n_tiles = d // tn
    row = lambda width: pl.BlockSpec((tm, width), lambda i, j: (i, 0))
    gate_maps = [functools.partial(_gate_block, layer=layers[0], branch=br, n_tiles=n_tiles) for br in range(4)]
    gate_w = [pl.BlockSpec((None, d, tn), gm) for gm in gate_maps]
    gate_b = [pl.BlockSpec((None, 1, tn), gm) for gm in gate_maps]
    br_w = [pl.BlockSpec((None, w.shape[1], tn), functools.partial(lambda i, j, l: (l, 0, j), l=l))
            for w, l in zip(w_brs, layers[1:])]
    b_gate3 = b_gate.reshape(b_gate.shape[0], 1, -1)
    return pl.pallas_call(
        _merge_kernel,
        grid=(m // tm, n_tiles),
        in_specs=[row(d)] + [row(y.shape[1]) for y in ys] + gate_w + gate_b + br_w,
        out_specs=pl.BlockSpec((tm, tn), lambda i, j: (i, j)),
        out_shape=jax.ShapeDtypeStruct((m, d), BF16),
        compiler_params=_params(2),
        name="gated_merge",
    )(h, *ys, w_gate, w_gate, w_gate, w_gate, b_gate3, b_gate3, b_gate3, b_gate3, *w_brs)


def _silu_mul(g, u):
    return g * jax.nn.sigmoid(g) * u


def _swiglu_up_kernel(x_ref, g_ref, wg_ref, wu_ref, o_ref, h_scr):
    @pl.when(pl.program_id(1) == 0)
    def _():
        h_scr[...] = _rmsnorm_body(x_ref[...], g_ref[...]).astype(h_scr.dtype)

    a = h_scr[...]
    o_ref[...] = _silu_mul(_dot(a, wg_ref[...].astype(BF16)), _dot(a, wu_ref[...].astype(BF16))).astype(o_ref.dtype)


def swiglu_up(x, norm_g, wg, wu, layer, tm=1024, tf=512):
    m, k = x.shape
    f = wg.shape[2]
    return pl.pallas_call(
        _swiglu_up_kernel,
        grid=(m // tm, f // tf),
        in_specs=[pl.BlockSpec((tm, k), lambda i, j: (i, 0)),
                  pl.BlockSpec((1, k), lambda i, j: (0, 0)),
                  pl.BlockSpec((None, k, tf), lambda i, j: (layer, 0, j)),
                  pl.BlockSpec((None, k, tf), lambda i, j: (layer, 0, j))],
        out_specs=pl.BlockSpec((tm, tf), lambda i, j: (i, j)),
        out_shape=jax.ShapeDtypeStruct((m, f), BF16),
        scratch_shapes=[pltpu.VMEM((tm, k), BF16)],
        compiler_params=_params(2),
        name="swiglu_up",
    )(x, norm_g.reshape(1, k), wg, wu)


def _router_kernel(x_ref, g_ref, r_ref, idx_ref, rank_ref, gate_ref, cnt_ref):
    @pl.when(pl.program_id(0) == 0)
    def _():
        cnt_ref[...] = jnp.zeros_like(cnt_ref)

    h = _rmsnorm_body(x_ref[...], g_ref[...])
    r = r_ref[...]
    h_hi, r_hi = h.astype(BF16), r.astype(BF16)
    h_lo, r_lo = (h - h_hi.astype(F32)).astype(BF16), (r - r_hi.astype(F32)).astype(BF16)
    logits = _dot(h_hi, r_hi) + (_dot(h_hi, r_lo) + _dot(h_lo, r_hi))
    lane = lax.broadcasted_iota(jnp.int32, logits.shape, 1).astype(F32)
    neg = -jnp.inf
    lg = jnp.where(lane < N_EXPERTS, logits, neg)
    v1 = jnp.max(lg, axis=-1, keepdims=True)
    i1 = jnp.min(jnp.where(lg == v1, lane, float(LANES)), axis=-1, keepdims=True)
    lg2 = jnp.where(lane == i1, neg, lg)
    v2 = jnp.max(lg2, axis=-1, keepdims=True)
    i2 = jnp.min(jnp.where(lg2 == v2, lane, float(LANES)), axis=-1, keepdims=True)
    e = jnp.exp(v2 - v1)
    g1 = 1.0 / (1.0 + e)
    g2 = e / (1.0 + e)
    onehot = jnp.where((lane == i1) | (lane == i2), 1.0, 0.0)
    tt = onehot.shape[0]
    earlier = lax.broadcasted_iota(jnp.int32, (tt, tt), 1) < lax.broadcasted_iota(jnp.int32, (tt, tt), 0)
    pos = _dot(jnp.where(earlier, 1.0, 0.0).astype(BF16), onehot.astype(BF16)) + cnt_ref[...]
    r1 = jnp.sum(jnp.where(lane == i1, pos, 0.0), axis=-1, keepdims=True)
    r2 = jnp.sum(jnp.where(lane == i2, pos, 0.0), axis=-1, keepdims=True)
    cnt_ref[...] += jnp.sum(onehot, axis=0, keepdims=True)
    idx_ref[...] = jnp.where(lane == 0, i1, jnp.where(lane == 1, i2, 0.0)).astype(jnp.int32)
    rank_ref[...] = jnp.where(lane == 0, r1, jnp.where(lane == 1, r2, 0.0)).astype(jnp.int32)
    gate_ref[...] = jnp.where(lane == 0, g1, jnp.where(lane == 1, g2, 0.0))


def route(x, g, router, tm=512):
    m, d = x.shape
    r_pad = jnp.pad(router, ((0, 0), (0, LANES - N_EXPERTS)))
    lane_spec = pl.BlockSpec((tm, LANES), lambda i: (i, 0))
    return pl.pallas_call(
        _router_kernel,
        grid=(m // tm,),
        in_specs=[pl.BlockSpec((tm, d), lambda i: (i, 0)),
                  pl.BlockSpec((1, d), lambda i: (0, 0)),
                  pl.BlockSpec((d, LANES), lambda i: (0, 0))],
        out_specs=[lane_spec, lane_spec, lane_spec, pl.BlockSpec((1, LANES), lambda i: (0, 0))],
        out_shape=[jax.ShapeDtypeStruct((m, LANES), jnp.int32),
                   jax.ShapeDtypeStruct((m, LANES), jnp.int32),
                   jax.ShapeDtypeStruct((m, LANES), F32),
                   jax.ShapeDtypeStruct((1, LANES), F32)],
        compiler_params=_params(1),
        name="route",
    )(x, g.reshape(1, d), r_pad)


def _dispatch_kernel(dest_ref, lo_ref, hi_ref, na_ref, x_hbm, g_ref, a_ref, src_ref, rows_ref, sems):
    j = pl.program_id(0)
    tm = rows_ref.shape[1]
    n_active = na_ref[0]

    def row_copy(tile, r):
        slot = lax.rem(tile, 2)
        return pltpu.make_async_copy(x_hbm.at[pl.ds(src_ref[tile * tm + r], 1)],
                                     rows_ref.at[slot, pl.ds(r, 1)], sems.at[slot])

    def start_tile(tile):
        def start(r2, c):
            for u in range(N_DMA_PRIORITIES):
                row_copy(tile, N_DMA_PRIORITIES * r2 + u).start(priority=u)
            return c

        lax.fori_loop(0, tm // N_DMA_PRIORITIES, start, 0, unroll=4)

    @pl.when(j == 0)
    def _():
        def clear(i, c):
            src_ref[i] = 0
            return c

        for e in range(N_EXPERTS):
            lax.fori_loop(lo_ref[e], hi_ref[e], clear, 0)

        def fill(i, c):
            src_ref[dest_ref[i]] = lax.shift_right_logical(i, 1)
            return c

        lax.fori_loop(0, dest_ref.shape[0], fill, 0, unroll=8)

        @pl.when(n_active > 0)
        def _():
            start_tile(0)

    @pl.when(j + 1 < n_active)
    def _():
        start_tile(j + 1)

    @pl.when(j < n_active)
    def _():
        def wait(r, c):
            row_copy(j, r).wait()
            return c

        lax.fori_loop(0, tm, wait, 0, unroll=8)
        a_ref[...] = _rmsnorm_body(rows_ref[lax.rem(j, 2)], g_ref[...]).astype(a_ref.dtype)

    @pl.when(j >= n_active)
    def _():
        a_ref[...] = jnp.zeros_like(a_ref)


def _combine_kernel(dest_ref, x_ref, gate_ref, g_ref, y_hbm, o_ref, rows_ref, sems, *, apply_norm):
    tt = x_ref.shape[0]
    i = pl.program_id(0)

    def row_copy(tile, r, k):
        slot = lax.rem(tile, 2)
        row = dest_ref[TOP_K * (tile * tt + r) + k]
        return pltpu.make_async_copy(y_hbm.at[pl.ds(row, 1)], rows_ref.at[slot, k, pl.ds(r, 1)], sems.at[slot])

    def start_tile(tile):
        def start(r, c):
            for k in range(TOP_K):
                row_copy(tile, r, k).start(priority=k % N_DMA_PRIORITIES)
            return c

        lax.fori_loop(0, tt, start, 0, unroll=4)

    @pl.when(i == 0)
    def _():
        start_tile(0)

    @pl.when(i + 1 < pl.num_programs(0))
    def _():
        start_tile(i + 1)

    def wait(r, c):
        for k in range(TOP_K):
            row_copy(i, r, k).wait()
        return c

    lax.fori_loop(0, tt, wait, 0, unroll=4)
    gates = gate_ref[...]
    slot = lax.rem(i, 2)
    y = x_ref[...]
    for k in range(TOP_K):
        y = y + gates[:, k:k + 1] * rows_ref[slot, k]
    o_ref[...] = _rmsnorm_body(y, g_ref[...]) if apply_norm else y


def _moe_up_kernel(te_ref, na_ref, a_ref, wg_ref, wu_ref, o_ref):
    @pl.when(pl.program_id(1) < na_ref[0])
    def _():
        a = a_ref[...]
        o_ref[...] = _silu_mul(_dot(a, wg_ref[...].astype(BF16)), _dot(a, wu_ref[...].astype(BF16))).astype(o_ref.dtype)

    @pl.when(pl.program_id(1) >= na_ref[0])
    def _():
        o_ref[...] = jnp.zeros_like(o_ref)


def _moe_down_kernel(te_ref, na_ref, a_ref, w_ref, o_ref, *, tk):
    @pl.when(pl.program_id(1) < na_ref[0])
    def _():
        acc = None
        for c in range(a_ref.shape[1] // tk):
            part = _dot(a_ref[:, c * tk:(c + 1) * tk], w_ref[c * tk:(c + 1) * tk, :].astype(BF16))
            acc = part if acc is None else acc + part
        o_ref[...] = acc

    @pl.when(pl.program_id(1) >= na_ref[0])
    def _():
        o_ref[...] = jnp.zeros_like(o_ref)


def moe_layer(x, norm_g, router, wg, wu, wd, layer, out_g, tm=512, tf=1024, tn=512, tk=1024, tt=256):
    assert TOP_K == 2
    t, d = x.shape
    _, n_exp, _, f = wg.shape
    n_rows = t * TOP_K
    n_tiles = n_rows // tm + n_exp
    idx, rank, gate, cnt = route(x, norm_g, router)
    counts = cnt[0, :n_exp].astype(jnp.int32)
    tiles_per = (counts + tm - 1) // tm
    tile_end = jnp.cumsum(tiles_per)
    start = (tile_end - tiles_per) * tm
    n_active = tile_end[-1:]
    tile_ids = jnp.arange(n_tiles, dtype=jnp.int32)
    tile_expert = jnp.sum((tile_ids[:, None] >= tile_end[None, :]).astype(jnp.int32), axis=1)
    last_expert = jnp.sum((n_active - 1 >= tile_end).astype(jnp.int32))
    tile_expert = jnp.where(tile_ids < n_active, tile_expert, last_expert).astype(jnp.int32)
    picked = idx[:, :TOP_K, None] == jnp.arange(n_exp, dtype=jnp.int32)
    dest = (jnp.sum(jnp.where(picked, start, 0), axis=-1) + rank[:, :TOP_K]).reshape(n_rows)

    row_tile = lambda j, na: jnp.maximum(jnp.minimum(j, na[0] - 1), 0)
    a = pl.pallas_call(
        _dispatch_kernel,
        grid_spec=pltpu.PrefetchScalarGridSpec(
            num_scalar_prefetch=4,
            grid=(n_tiles,),
            in_specs=[pl.BlockSpec(memory_space=pl.ANY),
                      pl.BlockSpec((1, d), lambda j, dst, lo, hi, na: (0, 0))],
            out_specs=pl.BlockSpec((tm, d), lambda j, dst, lo, hi, na: (j, 0)),
            scratch_shapes=[pltpu.SMEM((n_tiles * tm,), jnp.int32),
                            pltpu.VMEM((2, tm, d), F32),
                            pltpu.SemaphoreType.DMA((2,))]),
        out_shape=jax.ShapeDtypeStruct((n_tiles * tm, d), BF16),
        compiler_params=_params(1),
        name="moe_dispatch",
    )(dest, start + counts, tile_end * tm, n_active, x, norm_g.reshape(1, d))
    up = pl.pallas_call(
        _moe_up_kernel,
        grid_spec=pltpu.PrefetchScalarGridSpec(
            num_scalar_prefetch=2,
            grid=(f // tf, n_tiles),
            in_specs=[pl.BlockSpec((tm, d), lambda c, j, te, na: (row_tile(j, na), 0)),
                      pl.BlockSpec((None, None, d, tf), lambda c, j, te, na: (layer, te[j], 0, c)),
                      pl.BlockSpec((None, None, d, tf), lambda c, j, te, na: (layer, te[j], 0, c))],
            out_specs=pl.BlockSpec((tm, tf), lambda c, j, te, na: (j, c))),
        out_shape=jax.ShapeDtypeStruct((n_tiles * tm, f), BF16),
        compiler_params=_params(2),
        name="moe_up",
    )(tile_expert, n_active, a, wg, wu)
    down = pl.pallas_call(
        functools.partial(_moe_down_kernel, tk=tk),
        grid_spec=pltpu.PrefetchScalarGridSpec(
            num_scalar_prefetch=2,
            grid=(d // tn, n_tiles),
            in_specs=[pl.BlockSpec((tm, f), lambda c, j, te, na: (row_tile(j, na), 0)),
                      pl.BlockSpec((None, None, f, tn), lambda c, j, te, na: (layer, te[j], 0, c))],
            out_specs=pl.BlockSpec((tm, tn), lambda c, j, te, na: (j, c))),
        out_shape=jax.ShapeDtypeStruct((n_tiles * tm, d), F32),
        compiler_params=_params(2),
        name="moe_down",
    )(tile_expert, n_active, up, wd)
    g_out = jnp.ones((d,), F32) if out_g is None else out_g
    return pl.pallas_call(
        functools.partial(_combine_kernel, apply_norm=out_g is not None),
        grid_spec=pltpu.PrefetchScalarGridSpec(
            num_scalar_prefetch=1,
            grid=(t // tt,),
            in_specs=[pl.BlockSpec((tt, d), lambda i, dst: (i, 0)),
                      pl.BlockSpec((tt, LANES), lambda i, dst: (i, 0)),
                      pl.BlockSpec((1, d), lambda i, dst: (0, 0)),
                      pl.BlockSpec(memory_space=pl.ANY)],
            out_specs=pl.BlockSpec((tt, d), lambda i, dst: (i, 0)),
            scratch_shapes=[pltpu.VMEM((2, TOP_K, tt, d), F32),
                            pltpu.SemaphoreType.DMA((2,))]),
        out_shape=jax.ShapeDtypeStruct((t, d), F32),
        compiler_params=_params(1),
        name="moe_combine",
    )(dest, x, gate, g_out.reshape(1, d), down)


def _mixer(x, mem_n, b, s, layer, w_in, rpb, w_mem_kv, w_br_a, w_br_b, w_br_c, w_br_m, w_gate, b_gate, w_out,
           norm_g):
    t = b * s
    h = rmsnorm(x, norm_g, BF16)
    z = matmul(h, w_in, layer, tm=2048, tn=NA_W, name="in_proj").reshape(b, s, -1)
    kv_m = matmul(mem_n, w_mem_kv, layer, tm=mem_n.shape[0], tn=MEM_W, name="mem_kv_proj")
    y_a = neighbourhood_attention(z, rpb).reshape(t, NA_W)
    y_b = fourier_mix(z).reshape(t, FN_W)
    y_c = dilated_attention(z)
    y_m = memory_attention(z, kv_m.reshape(b, -1, 2 * MEM_W)).reshape(t, MEM_W)
    merged = gated_merge(h, (y_a, y_b, y_c, y_m), w_gate, b_gate, (w_br_a, w_br_b, w_br_c, w_br_m),
                         (layer,) * 5)
    return matmul(merged, w_out, layer, tm=2048, tn=512, out_dtype=F32, residual=x, name="out_proj")


def kernel(x, mem, norm_mix, w_in, rpb, norm_mem, w_mem_kv, w_br_a, w_br_b, w_br_c, w_br_m, w_gate, b_gate,
           w_out, norm_ffn, dense_w_gate, dense_w_up, dense_w_down, router, exp_w_gate, exp_w_up, exp_w_down,
           final_norm):
    b, s, d = x.shape
    depth = norm_mix.shape[0]
    t = b * s
    xf = x.reshape(t, d)
    memf = mem.reshape(-1, d)
    normed = False
    for layer in range(depth):
        mem_n = rmsnorm(memf, norm_mem[layer], BF16)
        xf = _mixer(xf, mem_n, b, s, layer, w_in, rpb[layer], w_mem_kv, w_br_a, w_br_b, w_br_c, w_br_m, w_gate,
                    b_gate, w_out, norm_mix[layer])
        i = layer // 2
        if layer % 2 == 0:
            u = swiglu_up(xf, norm_ffn[layer], dense_w_gate, dense_w_up, i)
            xf = matmul(u, dense_w_down, i, tm=1024, tn=256, out_dtype=F32, residual=xf, name="swiglu_down")
        else:
            normed = layer == depth - 1
            xf = moe_layer(xf, norm_ffn[layer], router[i], exp_w_gate, exp_w_up, exp_w_down, i,
                           final_norm if normed else None)
    if not normed:
        xf = rmsnorm(xf, final_norm, F32)
    return xf.reshape(b, s, d)
```

```python
import functools

import numpy as np
import jax
import jax.numpy as jnp
from jax import lax
from jax.experimental import pallas as pl
from jax.experimental.pallas import tpu as pltpu

F32 = jnp.float32
BF16 = jnp.bfloat16

D_MODEL = 2048
GRID_W = 64
HEAD_DIM = 128
RMS_EPS = 1e-6
NA_HEADS = 6
NA_KH = 8
NA_KW = 16
NA_W = NA_HEADS * HEAD_DIM
FN_GROUPS = 6
FN_GROUP_W = 128
FN_W = FN_GROUPS * FN_GROUP_W
DIL_DILATIONS = (1, 4, 16)
DIL_HALF_WINDOW = 64
DIL_HEADS_PER_GROUP = 2
DIL_GROUP_W = DIL_HEADS_PER_GROUP * HEAD_DIM
Q_BLOCK = 128
ROT_DIM = HEAD_DIM // 4
ROPE_THETA = 500000.0
MEM_HEADS = 4
MEM_HEAD_DIM = 192
MEM_W = MEM_HEADS * MEM_HEAD_DIM
MEM_WINDOW = 256
N_EXPERTS = 8
TOP_K = 2
MOE_ROW_QUANTUM = 128
MASK_VALUE = -1e30

V7X_VMEM_BYTES = 64 * 1024 * 1024
VMEM_LIMIT_BYTES = V7X_VMEM_BYTES - 4 * 1024 * 1024
LANES = 128
N_DMA_PRIORITIES = 2


def _params(n_axes):
    return pltpu.CompilerParams(dimension_semantics=("arbitrary",) * n_axes,
                                vmem_limit_bytes=VMEM_LIMIT_BYTES)


def _dot(a, b):
    return jnp.dot(a, b, preferred_element_type=F32)


def _dot_nt(a, b):
    return lax.dot_general(a, b, (((1,), (1,)), ((), ())), preferred_element_type=F32)


def _softmax_pv(s, v):
    d = v.shape[1]
    top = jnp.max(s, axis=-1, keepdims=True)
    p = jnp.exp((s - top).astype(BF16))
    pv = _dot(p, jnp.concatenate([v, jnp.ones_like(v)], axis=1))
    num, den = pv[:, :d], pv[:, d:]
    return num / den, top + jnp.log(den)


def _rmsnorm_body(x, g):
    return x * lax.rsqrt(jnp.mean(x * x, axis=-1, keepdims=True) + RMS_EPS) * g


def _rmsnorm_kernel(x_ref, g_ref, o_ref):
    o_ref[...] = _rmsnorm_body(x_ref[...], g_ref[...]).astype(o_ref.dtype)


def rmsnorm(x, g, out_dtype, tm=512):
    m, d = x.shape
    return pl.pallas_call(
        _rmsnorm_kernel,
        grid=(m // tm,),
        in_specs=[pl.BlockSpec((tm, d), lambda i: (i, 0)),
                  pl.BlockSpec((1, d), lambda i: (0, 0))],
        out_specs=pl.BlockSpec((tm, d), lambda i: (i, 0)),
        out_shape=jax.ShapeDtypeStruct((m, d), out_dtype),
        compiler_params=_params(1),
        name="rmsnorm",
    )(x, g.reshape(1, d))


def _mm_kernel(a_ref, w_ref, o_ref):
    o_ref[...] = _dot(a_ref[...], w_ref[...].astype(BF16)).astype(o_ref.dtype)


def _mm_res_kernel(a_ref, w_ref, r_ref, o_ref):
    o_ref[...] = (r_ref[...] + _dot(a_ref[...], w_ref[...].astype(BF16))).astype(o_ref.dtype)


def matmul(a, w, layer, *, tm, tn, n_cols=None, out_dtype=BF16, residual=None, name="matmul"):
    m, k = a.shape
    n = w.shape[2] if n_cols is None else n_cols
    in_specs = [pl.BlockSpec((tm, k), lambda i, j: (i, 0)),
                pl.BlockSpec((None, k, tn), lambda i, j: (layer, 0, j))]
    args = [a, w]
    kernel = _mm_kernel
    if residual is not None:
        in_specs.append(pl.BlockSpec((tm, tn), lambda i, j: (i, j)))
        args.append(residual)
        kernel = _mm_res_kernel
    return pl.pallas_call(
        kernel,
        grid=(m // tm, n // tn),
        in_specs=in_specs,
        out_specs=pl.BlockSpec((tm, tn), lambda i, j: (i, j)),
        out_shape=jax.ShapeDtypeStruct((m, n), out_dtype),
        compiler_params=_params(2),
        name=name,
    )(*args)


NA_Q_ROWS = 4
NA_KEY_ROWS = NA_KH + NA_Q_ROWS


def _na_key_row_start(step, rows):
    return np.clip(step * NA_Q_ROWS - NA_KH // 2, 0, rows - NA_KEY_ROWS)


def _na_bias_table(rpb, rows):
    col = np.arange(GRID_W)
    col_start = np.clip(col - NA_KW // 2, 0, GRID_W - NA_KW)
    rel_col = col[None, :] - col[:, None] + (NA_KW - 1)
    col_ok = (col[None, :] >= col_start[:, None]) & (col[None, :] < col_start[:, None] + NA_KW)
    pick = (np.clip(rel_col, 0, 2 * NA_KW - 2)[None] == np.arange(2 * NA_KW - 1)[:, None, None]).astype(np.float32)
    slab = jnp.einsum("hdr,rcw->hdcw", rpb.astype(F32), pick, precision=lax.Precision.HIGHEST)
    n_steps = rows // NA_Q_ROWS
    rel_row = np.zeros((n_steps, NA_Q_ROWS, NA_KEY_ROWS), np.int64)
    row_ok = np.zeros((n_steps, NA_Q_ROWS, NA_KEY_ROWS), bool)
    for step in range(n_steps):
        for i in range(NA_Q_ROWS):
            r = step * NA_Q_ROWS + i
            win = np.clip(r - NA_KH // 2, 0, rows - NA_KH)
            key_row = _na_key_row_start(step, rows) + np.arange(NA_KEY_ROWS)
            row_ok[step, i] = (key_row >= win) & (key_row < win + NA_KH)
            rel_row[step, i] = key_row - r + (NA_KH - 1)
    assert all((rel_row[s_] * row_ok[s_] == rel_row[1] * row_ok[1]).all() and (row_ok[s_] == row_ok[1]).all()
               for s_ in range(1, n_steps - 1))
    slab = jnp.where(col_ok, slab, MASK_VALUE)
    hidden = jnp.full(slab.shape[:1] + slab.shape[2:], MASK_VALUE, F32)
    cases = []
    for step in (0, 1, n_steps - 1):
        q_rows = [jnp.concatenate([slab[:, rel_row[step, i, k]] if row_ok[step, i, k] else hidden
                                   for k in range(NA_KEY_ROWS)], axis=-1) for i in range(NA_Q_ROWS)]
        cases.append(jnp.concatenate(q_rows, axis=-2))
    return jnp.stack(cases)


def _na_kernel(q_ref, k_ref, v_ref, b_ref, o_ref, *, rows):
    step = pl.program_id(1)
    key_row0 = jnp.clip(step * NA_Q_ROWS - NA_KH // 2, 0, rows - NA_KEY_ROWS)
    k0 = pl.multiple_of(key_row0 * GRID_W, GRID_W)
    n_keys = NA_KEY_ROWS * GRID_W
    scale = HEAD_DIM ** -0.5
    for h in range(NA_HEADS):
        cols = slice(h * HEAD_DIM, (h + 1) * HEAD_DIM)
        k = k_ref[0, pl.ds(k0, n_keys), cols]
        v = v_ref[0, pl.ds(k0, n_keys), cols]
        s = _dot_nt(q_ref[0, :, cols], k) * scale + b_ref[0, h]
        o_ref[0, :, cols] = _softmax_pv(s, v)[0].astype(o_ref.dtype)


def neighbourhood_attention(z, rpb):
    b, s, _ = z.shape
    rows = s // GRID_W
    n_steps = rows // NA_Q_ROWS
    bias = _na_bias_table(rpb, rows)
    tq = NA_Q_ROWS * GRID_W
    bias_case = lambda i, r: (jnp.where(r == 0, 0, jnp.where(r == n_steps - 1, 2, 1)), 0, 0, 0)
    return pl.pallas_call(
        functools.partial(_na_kernel, rows=rows),
        grid=(b, n_steps),
        in_specs=[pl.BlockSpec((1, tq, NA_W), lambda i, r: (i, r, 0)),
                  pl.BlockSpec((1, s, NA_W), lambda i, r: (i, 0, 1)),
                  pl.BlockSpec((1, s, NA_W), lambda i, r: (i, 0, 2)),
                  pl.BlockSpec((1,) + bias.shape[1:], bias_case)],
        out_specs=pl.BlockSpec((1, tq, NA_W), lambda i, r: (i, r, 0)),
        out_shape=jax.ShapeDtypeStruct((b, s, NA_W), BF16),
        compiler_params=_params(2),
        name="neighbourhood_attention",
    )(z, z, z, bias)


def _dft_tables(s):
    half = s // 2
    n = np.arange(half, dtype=np.int64)
    ang = 2.0 * np.pi * ((n[:, None] * n[None, :]) % half) / half
    pos = np.concatenate([np.cos(ang), -np.sin(ang)], axis=1)
    c = np.arange(FN_GROUP_W, dtype=np.int64)
    ang_c = 2.0 * np.pi * ((c[:, None] * c[None, :]) % FN_GROUP_W) / FN_GROUP_W
    chan = np.concatenate([np.cos(ang_c), np.sin(ang_c)], axis=1)
    tw = np.pi * n[:, None] / half * np.ones((1, FN_GROUP_W))
    return (jnp.asarray(pos, F32).astype(BF16), jnp.asarray(chan, F32).astype(BF16),
            jnp.asarray(np.cos(tw), F32), jnp.asarray(np.sin(tw), F32))


def _fourier_kernel(u_ref, chan_ref, twc_ref, tws_ref, pos_ref, o_ref, ve_scr, vo_scr, y_scr, *, s, norm):
    half = s // 2

    @pl.when(pl.program_id(1) == 0)
    def _():
        twc, tws = twc_ref[...], tws_ref[...]
        for g in range(FN_GROUPS):
            cols = slice(g * FN_GROUP_W, (g + 1) * FN_GROUP_W)
            t = _dot(u_ref[0, :, cols], chan_ref[...])
            top, bot = t[:half], t[half:]
            fold = top + bot
            diff = top - bot
            dc, ds = diff[:, :FN_GROUP_W], diff[:, FN_GROUP_W:]
            ve_scr[0:half, cols] = fold[:, :FN_GROUP_W].astype(BF16)
            ve_scr[half:s, cols] = fold[:, FN_GROUP_W:].astype(BF16)
            vo_scr[0:half, cols] = (twc * dc - tws * ds).astype(BF16)
            vo_scr[half:s, cols] = (tws * dc + twc * ds).astype(BF16)

    rows = pos_ref.shape[0]
    y_even = _dot(pos_ref[...], ve_scr[...]) * norm
    y_odd = _dot(pos_ref[...], vo_scr[...]) * norm
    for g in range(FN_GROUPS):
        cols = slice(g * FN_GROUP_W, (g + 1) * FN_GROUP_W)
        y_scr[g, pl.ds(0, rows, stride=2), :] = y_even[:, cols]
        y_scr[g, pl.ds(1, rows, stride=2), :] = y_odd[:, cols]
        o_ref[0, :, cols] = y_scr[g].astype(o_ref.dtype)


def fourier_mix(z, ts=512):
    b, s, _ = z.shape
    pos_tab, chan_tab, tw_cos, tw_sin = _dft_tables(s)
    norm = float((s * FN_GROUP_W) ** -0.5)
    whole = lambda a: pl.BlockSpec(a.shape, lambda i, j: (0, 0))
    return pl.pallas_call(
        functools.partial(_fourier_kernel, s=s, norm=norm),
        grid=(b, s // (2 * ts)),
        in_specs=[pl.BlockSpec((1, s, FN_W), lambda i, j: (i, 0, 6)),
                  whole(chan_tab), whole(tw_cos), whole(tw_sin),
                  pl.BlockSpec((ts, s), lambda i, j: (j, 0))],
        out_specs=pl.BlockSpec((1, 2 * ts, FN_W), lambda i, j: (i, j, 0)),
        out_shape=jax.ShapeDtypeStruct((b, s, FN_W), BF16),
        scratch_shapes=[pltpu.VMEM((s, FN_W), BF16), pltpu.VMEM((s, FN_W), BF16),
                        pltpu.VMEM((FN_GROUPS, 2 * ts, FN_GROUP_W), F32)],
        compiler_params=_params(2),
        name="fourier_mix",
    )(z, chan_tab, tw_cos, tw_sin, pos_tab)


def _rope_tables(s):
    half = ROT_DIM // 2
    inv_freq = ROPE_THETA ** (-2.0 * jnp.arange(half, dtype=F32) / ROT_DIM)
    ang = jnp.arange(s, dtype=jnp.int32).astype(F32)[:, None] * inv_freq[None, :]
    cos, sin = jnp.cos(ang), jnp.sin(ang)
    ones = jnp.ones((s, HEAD_DIM - ROT_DIM), F32)
    zeros_half = jnp.zeros((s, half), F32)
    zeros_rest = jnp.zeros((s, HEAD_DIM - ROT_DIM), F32)
    cos_t = jnp.concatenate([cos, cos, ones], axis=1)
    sin_lo = jnp.concatenate([-sin, zeros_half, zeros_rest], axis=1)
    sin_hi = jnp.concatenate([zeros_half, sin, zeros_rest], axis=1)
    return cos_t, sin_lo, sin_hi


def _dilated_kernel(q0, q1, q2, k0, k1, k2, v0, v1, v2, cos_ref, slo_ref, shi_ref, y_ref,
                    rq_ref, rk_ref, rv_ref, o_scr, l_scr, *, seq):
    half = ROT_DIM // 2
    scale = HEAD_DIM ** -0.5
    chunk = 4 * Q_BLOCK
    for g, (q_ref, k_ref, v_ref) in enumerate(((q0, k0, v0), (q1, k1, v1), (q2, k2, v2))):
        for c0 in range(0, seq, chunk):
            rows = slice(c0, c0 + chunk)
            cs, lo, hi = cos_ref[rows, :], slo_ref[rows, :], shi_ref[rows, :]
            for src, dst in ((q_ref, rq_ref), (k_ref, rk_ref)):
                x = src[0, rows, :].astype(F32)
                dst[g, rows, :] = x * cs + pltpu.roll(x, HEAD_DIM - half, 1) * lo + pltpu.roll(x, half, 1) * hi
            rv_ref[g, rows, :] = v_ref[0, rows, :].astype(F32)

    masks = {}

    def window_mask(n_keys, offset):
        if (n_keys, offset) not in masks:
            rel = (lax.broadcasted_iota(jnp.int32, (Q_BLOCK, n_keys), 1)
                   - lax.broadcasted_iota(jnp.int32, (Q_BLOCK, n_keys), 0) + offset)
            masks[(n_keys, offset)] = jnp.abs(rel) <= DIL_HALF_WINDOW
        return masks[(n_keys, offset)]

    for g, d in enumerate(DIL_DILATIONS):
        length = seq // d
        n_keys = min(length, Q_BLOCK + 2 * DIL_HALF_WINDOW)
        for rho in range(d):
            for m_q in range(0, length, Q_BLOCK):
                m_k = min(max(m_q - DIL_HALF_WINDOW, 0), length - n_keys)
                q_rows = pl.ds(rho + d * m_q, Q_BLOCK, stride=d) if d > 1 else pl.ds(m_q, Q_BLOCK)
                k_rows = pl.ds(rho + d * m_k, n_keys, stride=d) if d > 1 else pl.ds(m_k, n_keys)
                q = rq_ref[g, q_rows, :].astype(BF16)
                k = rk_ref[g, k_rows, :].astype(BF16)
                v = rv_ref[g, k_rows, :].astype(BF16)
                s = jnp.where(window_mask(n_keys, m_k - m_q), _dot_nt(q, k) * scale, MASK_VALUE)
                o_scr[g, q_rows, :], l_scr[g, q_rows, :] = _softmax_pv(s, v)

    n_groups = len(DIL_DILATIONS)
    for c0 in range(0, seq, chunk):
        rows = slice(c0, c0 + chunk)
        lses = [l_scr[g, rows, :] for g in range(n_groups)]
        top = functools.reduce(jnp.maximum, lses)
        ws = [jnp.exp(l - top) for l in lses]
        num = sum(w * o_scr[g, rows, :] for g, w in enumerate(ws))
        y_ref[0, rows, :] = (num / sum(ws)).astype(y_ref.dtype)


def dilated_attention(z):
    b, s, _ = z.shape
    n_groups = len(DIL_DILATIONS)
    tables = _rope_tables(s)
    heads_per_block = NA_W // HEAD_DIM

    def head_spec(t, g):
        first = (3 + t) * heads_per_block + g * DIL_HEADS_PER_GROUP
        return pl.BlockSpec((1, s, HEAD_DIM), lambda i, h: (i, 0, first + h))

    table_spec = pl.BlockSpec((s, HEAD_DIM), lambda i, h: (0, 0))
    scratch = pltpu.VMEM((n_groups, s, HEAD_DIM), F32)
    return pl.pallas_call(
        functools.partial(_dilated_kernel, seq=s),
        grid=(b, DIL_HEADS_PER_GROUP),
        in_specs=[head_spec(t, g) for t in range(3) for g in range(n_groups)] + [table_spec] * 3,
        out_specs=pl.BlockSpec((1, s, HEAD_DIM), lambda i, h: (i, 0, h)),
        out_shape=jax.ShapeDtypeStruct((b, s, DIL_GROUP_W), BF16),
        scratch_shapes=[scratch] * 5,
        compiler_params=_params(2),
        name="dilated_attention",
    )(*([z] * 9), *tables).reshape(b * s, DIL_GROUP_W)


def _mem_attn_kernel(q_ref, k_ref, v_ref, o_ref):
    scale = MEM_HEAD_DIM ** -0.5
    col = lax.broadcasted_iota(jnp.int32, (1, MEM_WINDOW), 1)
    parts = []
    for h in range(MEM_HEADS):
        start = h * MEM_HEAD_DIM // LANES * LANES
        win = slice(start, start + MEM_WINDOW)
        in_head = (col >= h * MEM_HEAD_DIM - start) & (col < (h + 1) * MEM_HEAD_DIM - start)
        q = q_ref[0, :, win]
        s = _dot_nt(jnp.where(in_head, q, jnp.zeros_like(q)), k_ref[0, :, win]) * scale
        o = jnp.where(in_head, _softmax_pv(s, v_ref[0, :, win])[0], 0.0)
        pads = (start, MEM_W - start - MEM_WINDOW)
        left, right = (jnp.zeros((o.shape[0], n), F32) for n in pads)
        parts.append(jnp.concatenate([a for a in (left, o, right) if a.shape[1]], axis=1))
    o_ref[0] = sum(parts).astype(o_ref.dtype)


def memory_attention(z, kv, tq=512):
    b, s, _ = z.shape
    m = kv.shape[1]
    return pl.pallas_call(
        _mem_attn_kernel,
        grid=(b, s // tq),
        in_specs=[pl.BlockSpec((1, tq, MEM_W), lambda i, j: (i, j, 7)),
                  pl.BlockSpec((1, m, MEM_W), lambda i, j: (i, 0, 0)),
                  pl.BlockSpec((1, m, MEM_W), lambda i, j: (i, 0, 1))],
        out_specs=pl.BlockSpec((1, tq, MEM_W), lambda i, j: (i, j, 0)),
        out_shape=jax.ShapeDtypeStruct((b, s, MEM_W), BF16),
        compiler_params=_params(2),
        name="memory_attention",
    )(z, kv, kv)


def _merge_kernel(h_ref, ya_ref, yb_ref, yc_ref, ym_ref, wg0, wg1, wg2, wg3, bg0, bg1, bg2, bg3,
                  wa_ref, wb_ref, wc_ref, wm_ref, o_ref):
    h = h_ref[...]
    acc = None
    for y_ref, wg, bg, wbr in ((ya_ref, wg0, bg0, wa_ref), (yb_ref, wg1, bg1, wb_ref),
                               (yc_ref, wg2, bg2, wc_ref), (ym_ref, wg3, bg3, wm_ref)):
        gate = jax.nn.sigmoid(_dot(h, wg[...].astype(BF16)) + bg[...])
        term = gate * _dot(y_ref[...], wbr[...].astype(BF16))
        acc = term if acc is None else acc + term
    o_ref[...] = acc.astype(o_ref.dtype)


def _gate_block(i, j, *, layer, branch, n_tiles):
    return (layer, 0, branch * n_tiles + j)


def gated_merge(h, ys, w_gate, b_gate, w_brs, layers, tm=1024, tn=256):
    m, d = h.shape
    n_tiles = d // tn
    row = lambda width: pl.BlockSpec((tm, width), lambda i, j: (i, 0))
    gate_maps = [functools.partial(_gate_block, layer=layers[0], branch=br, n_tiles=n_tiles) for br in range(4)]
    gate_w = [pl.BlockSpec((None, d, tn), gm) for gm in gate_maps]
    gate_b = [pl.BlockSpec((None, 1, tn), gm) for gm in gate_maps]
    br_w = [pl.BlockSpec((None, w.shape[1], tn), functools.partial(lambda i, j, l: (l, 0, j), l=l))
            for w, l in zip(w_brs, layers[1:])]
    b_gate3 = b_gate.reshape(b_gate.shape[0], 1, -1)
    return pl.pallas_call(
        _merge_kernel,
        grid=(m // tm, n_tiles),
        in_specs=[row(d)] + [row(y.shape[1]) for y in ys] + gate_w + gate_b + br_w,
        out_specs=pl.BlockSpec((tm, tn), lambda i, j: (i, j)),
        out_shape=jax.ShapeDtypeStruct((m, d), BF16),
        compiler_params=_params(2),
        name="gated_merge",
    )(h, *ys, w_gate, w_gate, w_gate, w_gate, b_gate3, b_gate3, b_gate3, b_gate3, *w_brs)


def _silu_mul(g, u):
    return g * jax.nn.sigmoid(g) * u


def _swiglu_up_kernel(x_ref, g_ref, wg_ref, wu_ref, o_ref, h_scr):
    @pl.when(pl.program_id(1) == 0)
    def _():
        h_scr[...] = _rmsnorm_body(x_ref[...], g_ref[...]).astype(h_scr.dtype)

    a = h_scr[...]
    o_ref[...] = _silu_mul(_dot(a, wg_ref[...].astype(BF16)), _dot(a, wu_ref[...].astype(BF16))).astype(o_ref.dtype)


def swiglu_up(x, norm_g, wg, wu, layer, tm=1024, tf=512):
    m, k = x.shape
    f = wg.shape[2]
    return pl.pallas_call(
        _swiglu_up_kernel,
        grid=(m // tm, f // tf),
        in_specs=[pl.BlockSpec((tm, k), lambda i, j: (i, 0)),
                  pl.BlockSpec((1, k), lambda i, j: (0, 0)),
                  pl.BlockSpec((None, k, tf), lambda i, j: (layer, 0, j)),
                  pl.BlockSpec((None, k, tf), lambda i, j: (layer, 0, j))],
        out_specs=pl.BlockSpec((tm, tf), lambda i, j: (i, j)),
        out_shape=jax.ShapeDtypeStruct((m, f), BF16),
        scratch_shapes=[pltpu.VMEM((tm, k), BF16)],
        compiler_params=_params(2),
        name="swiglu_up",
    )(x, norm_g.reshape(1, k), wg, wu)


def _router_kernel(x_ref, g_ref, r_ref, idx_ref, rank_ref, gate_ref, cnt_ref):
    @pl.when(pl.program_id(0) == 0)
    def _():
        cnt_ref[...] = jnp.zeros_like(cnt_ref)

    h = _rmsnorm_body(x_ref[...], g_ref[...])
    r = r_ref[...]
    h_hi, r_hi = h.astype(BF16), r.astype(BF16)
    h_lo, r_lo = (h - h_hi.astype(F32)).astype(BF16), (r - r_hi.astype(F32)).astype(BF16)
    logits = _dot(h_hi, r_hi) + (_dot(h_hi, r_lo) + _dot(h_lo, r_hi))
    lane = lax.broadcasted_iota(jnp.int32, logits.shape, 1).astype(F32)
    neg = -jnp.inf
    lg = jnp.where(lane < N_EXPERTS, logits, neg)
    v1 = jnp.max(lg, axis=-1, keepdims=True)
    i1 = jnp.min(jnp.where(lg == v1, lane, float(LANES)), axis=-1, keepdims=True)
    lg2 = jnp.where(lane == i1, neg, lg)
    v2 = jnp.max(lg2, axis=-1, keepdims=True)
    i2 = jnp.min(jnp.where(lg2 == v2, lane, float(LANES)), axis=-1, keepdims=True)
    e = jnp.exp(v2 - v1)
    g1 = 1.0 / (1.0 + e)
    g2 = e / (1.0 + e)
    onehot = jnp.where((lane == i1) | (lane == i2), 1.0, 0.0)
    tt = onehot.shape[0]
    earlier = lax.broadcasted_iota(jnp.int32, (tt, tt), 1) < lax.broadcasted_iota(jnp.int32, (tt, tt), 0)
    pos = _dot(jnp.where(earlier, 1.0, 0.0).astype(BF16), onehot.astype(BF16)) + cnt_ref[...]
    r1 = jnp.sum(jnp.where(lane == i1, pos, 0.0), axis=-1, keepdims=True)
    r2 = jnp.sum(jnp.where(lane == i2, pos, 0.0), axis=-1, keepdims=True)
    cnt_ref[...] += jnp.sum(onehot, axis=0, keepdims=True)
    idx_ref[...] = jnp.where(lane == 0, i1, jnp.where(lane == 1, i2, 0.0)).astype(jnp.int32)
    rank_ref[...] = jnp.where(lane == 0, r1, jnp.where(lane == 1, r2, 0.0)).astype(jnp.int32)
    gate_ref[...] = jnp.where(lane == 0, g1, jnp.where(lane == 1, g2, 0.0))


def route(x, g, router, tm=512):
    m, d = x.shape
    r_pad = jnp.pad(router, ((0, 0), (0, LANES - N_EXPERTS)))
    lane_spec = pl.BlockSpec((tm, LANES), lambda i: (i, 0))
    return pl.pallas_call(
        _router_kernel,
        grid=(m // tm,),
        in_specs=[pl.BlockSpec((tm, d), lambda i: (i, 0)),
                  pl.BlockSpec((1, d), lambda i: (0, 0)),
                  pl.BlockSpec((d, LANES), lambda i: (0, 0))],
        out_specs=[lane_spec, lane_spec, lane_spec, pl.BlockSpec((1, LANES), lambda i: (0, 0))],
        out_shape=[jax.ShapeDtypeStruct((m, LANES), jnp.int32),
                   jax.ShapeDtypeStruct((m, LANES), jnp.int32),
                   jax.ShapeDtypeStruct((m, LANES), F32),
                   jax.ShapeDtypeStruct((1, LANES), F32)],
        compiler_params=_params(1),
        name="route",
    )(x, g.reshape(1, d), r_pad)


def _dispatch_kernel(dest_ref, lo_ref, hi_ref, na_ref, x_hbm, g_ref, a_ref, src_ref, rows_ref, sems):
    j = pl.program_id(0)
    tm = rows_ref.shape[1]
    n_active = na_ref[0]

    def row_copy(tile, r):
        slot = lax.rem(tile, 2)
        return pltpu.make_async_copy(x_hbm.at[pl.ds(src_ref[tile * tm + r], 1)],
                                     rows_ref.at[slot, pl.ds(r, 1)], sems.at[slot])

    def start_tile(tile):
        def start(r2, c):
            for u in range(N_DMA_PRIORITIES):
                row_copy(tile, N_DMA_PRIORITIES * r2 + u).start(priority=u)
            return c

        lax.fori_loop(0, tm // N_DMA_PRIORITIES, start, 0, unroll=4)

    @pl.when(j == 0)
    def _():
        def clear(i, c):
            src_ref[i] = 0
            return c

        for e in range(N_EXPERTS):
            lax.fori_loop(lo_ref[e], hi_ref[e], clear, 0)

        def fill(i, c):
            src_ref[dest_ref[i]] = lax.shift_right_logical(i, 1)
            return c

        lax.fori_loop(0, dest_ref.shape[0], fill, 0, unroll=8)

        @pl.when(n_active > 0)
        def _():
            start_tile(0)

    @pl.when(j + 1 < n_active)
    def _():
        start_tile(j + 1)

    @pl.when(j < n_active)
    def _():
        def wait(r, c):
            row_copy(j, r).wait()
            return c

        lax.fori_loop(0, tm, wait, 0, unroll=8)
        a_ref[...] = _rmsnorm_body(rows_ref[lax.rem(j, 2)], g_ref[...]).astype(a_ref.dtype)

    @pl.when(j >= n_active)
    def _():
        a_ref[...] = jnp.zeros_like(a_ref)


def _combine_kernel(dest_ref, x_ref, gate_ref, g_ref, y_hbm, o_ref, rows_ref, sems, *, apply_norm):
    tt = x_ref.shape[0]
    i = pl.program_id(0)

    def row_copy(tile, r, k):
        slot = lax.rem(tile, 2)
        row = dest_ref[TOP_K * (tile * tt + r) + k]
        return pltpu.make_async_copy(y_hbm.at[pl.ds(row, 1)], rows_ref.at[slot, k, pl.ds(r, 1)], sems.at[slot])

    def start_tile(tile):
        def start(r, c):
            for k in range(TOP_K):
                row_copy(tile, r, k).start(priority=k % N_DMA_PRIORITIES)
            return c

        lax.fori_loop(0, tt, start, 0, unroll=4)

    @pl.when(i == 0)
    def _():
        start_tile(0)

    @pl.when(i + 1 < pl.num_programs(0))
    def _():
        start_tile(i + 1)

    def wait(r, c):
        for k in range(TOP_K):
            row_copy(i, r, k).wait()
        return c

    lax.fori_loop(0, tt, wait, 0, unroll=4)
    gates = gate_ref[...]
    slot = lax.rem(i, 2)
    y = x_ref[...]
    for k in range(TOP_K):
        y = y + gates[:, k:k + 1] * rows_ref[slot, k]
    o_ref[...] = _rmsnorm_body(y, g_ref[...]) if apply_norm else y


def _on_valid_rows(n_valid, o_ref, compute):
    tm = o_ref.shape[0]
    for n in range(MOE_ROW_QUANTUM, tm + 1, MOE_ROW_QUANTUM):
        @pl.when((n_valid > n - MOE_ROW_QUANTUM) & (n_valid <= n))
        def _(n=n):
            o_ref[:n, :] = compute(n).astype(o_ref.dtype)
            if n < tm:
                o_ref[n:, :] = jnp.zeros((tm - n, o_ref.shape[1]), o_ref.dtype)

    @pl.when(n_valid == 0)
    def _():
        o_ref[...] = jnp.zeros_like(o_ref)


def _moe_up_kernel(te_ref, na_ref, nv_ref, a_ref, wg_ref, wu_ref, o_ref):
    def compute(n):
        a = a_ref[:n, :]
        return _silu_mul(_dot(a, wg_ref[...].astype(BF16)), _dot(a, wu_ref[...].astype(BF16)))

    _on_valid_rows(nv_ref[pl.program_id(1)], o_ref, compute)


def _moe_down_kernel(te_ref, na_ref, nv_ref, a_ref, w_ref, o_ref, *, tk):
    def compute(n):
        acc = None
        for c in range(a_ref.shape[1] // tk):
            part = _dot(a_ref[:n, c * tk:(c + 1) * tk], w_ref[c * tk:(c + 1) * tk, :].astype(BF16))
            acc = part if acc is None else acc + part
        return acc

    _on_valid_rows(nv_ref[pl.program_id(1)], o_ref, compute)


def moe_layer(x, norm_g, router, wg, wu, wd, layer, out_g, tm=512, tf=1024, tn=512, tk=1024, tt=256):
    assert TOP_K == 2
    t, d = x.shape
    _, n_exp, _, f = wg.shape
    n_rows = t * TOP_K
    n_tiles = n_rows // tm + n_exp
    idx, rank, gate, cnt = route(x, norm_g, router)
    counts = cnt[0, :n_exp].astype(jnp.int32)
    tiles_per = (counts + tm - 1) // tm
    tile_end = jnp.cumsum(tiles_per)
    start = (tile_end - tiles_per) * tm
    n_active = tile_end[-1:]
    tile_ids = jnp.arange(n_tiles, dtype=jnp.int32)
    tile_expert = jnp.sum((tile_ids[:, None] >= tile_end[None, :]).astype(jnp.int32), axis=1)
    last_expert = jnp.sum((n_active - 1 >= tile_end).astype(jnp.int32))
    tile_expert = jnp.where(tile_ids < n_active, tile_expert, last_expert).astype(jnp.int32)
    group_end = start + counts
    tile_valid = jnp.where(tile_ids < n_active, jnp.clip(group_end[tile_expert] - tile_ids * tm, 0, tm), 0)
    picked = idx[:, :TOP_K, None] == jnp.arange(n_exp, dtype=jnp.int32)
    dest = (jnp.sum(jnp.where(picked, start, 0), axis=-1) + rank[:, :TOP_K]).reshape(n_rows)

    row_tile = lambda j, na: jnp.maximum(jnp.minimum(j, na[0] - 1), 0)
    a = pl.pallas_call(
        _dispatch_kernel,
        grid_spec=pltpu.PrefetchScalarGridSpec(
            num_scalar_prefetch=4,
            grid=(n_tiles,),
            in_specs=[pl.BlockSpec(memory_space=pl.ANY),
                      pl.BlockSpec((1, d), lambda j, dst, lo, hi, na: (0, 0))],
            out_specs=pl.BlockSpec((tm, d), lambda j, dst, lo, hi, na: (j, 0)),
            scratch_shapes=[pltpu.SMEM((n_tiles * tm,), jnp.int32),
                            pltpu.VMEM((2, tm, d), F32),
                            pltpu.SemaphoreType.DMA((2,))]),
        out_shape=jax.ShapeDtypeStruct((n_tiles * tm, d), BF16),
        compiler_params=_params(1),
        name="moe_dispatch",
    )(dest, group_end, tile_end * tm, n_active, x, norm_g.reshape(1, d))
    up = pl.pallas_call(
        _moe_up_kernel,
        grid_spec=pltpu.PrefetchScalarGridSpec(
            num_scalar_prefetch=3,
            grid=(f // tf, n_tiles),
            in_specs=[pl.BlockSpec((tm, d), lambda c, j, te, na, nv: (row_tile(j, na), 0)),
                      pl.BlockSpec((None, None, d, tf), lambda c, j, te, na, nv: (layer, te[j], 0, c)),
                      pl.BlockSpec((None, None, d, tf), lambda c, j, te, na, nv: (layer, te[j], 0, c))],
            out_specs=pl.BlockSpec((tm, tf), lambda c, j, te, na, nv: (j, c))),
        out_shape=jax.ShapeDtypeStruct((n_tiles * tm, f), BF16),
        compiler_params=_params(2),
        name="moe_up",
    )(tile_expert, n_active, tile_valid, a, wg, wu)
    down = pl.pallas_call(
        functools.partial(_moe_down_kernel, tk=tk),
        grid_spec=pltpu.PrefetchScalarGridSpec(
            num_scalar_prefetch=3,
            grid=(d // tn, n_tiles),
            in_specs=[pl.BlockSpec((tm, f), lambda c, j, te, na, nv: (row_tile(j, na), 0)),
                      pl.BlockSpec((None, None, f, tn), lambda c, j, te, na, nv: (layer, te[j], 0, c))],
            out_specs=pl.BlockSpec((tm, tn), lambda c, j, te, na, nv: (j, c))),
        out_shape=jax.ShapeDtypeStruct((n_tiles * tm, d), F32),
        compiler_params=_params(2),
        name="moe_down",
    )(tile_expert, n_active, tile_valid, up, wd)
    g_out = jnp.ones((d,), F32) if out_g is None else out_g
    return pl.pallas_call(
        functools.partial(_combine_kernel, apply_norm=out_g is not None),
        grid_spec=pltpu.PrefetchScalarGridSpec(
            num_scalar_prefetch=1,
            grid=(t // tt,),
            in_specs=[pl.BlockSpec((tt, d), lambda i, dst: (i, 0)),
                      pl.BlockSpec((tt, LANES), lambda i, dst: (i, 0)),
                      pl.BlockSpec((1, d), lambda i, dst: (0, 0)),
                      pl.BlockSpec(memory_space=pl.ANY)],
            out_specs=pl.BlockSpec((tt, d), lambda i, dst: (i, 0)),
            scratch_shapes=[pltpu.VMEM((2, TOP_K, tt, d), F32),
                            pltpu.SemaphoreType.DMA((2,))]),
        out_shape=jax.ShapeDtypeStruct((t, d), F32),
        compiler_params=_params(1),
        name="moe_combine",
    )(dest, x, gate, g_out.reshape(1, d), down)


def _mixer(x, mem_n, b, s, layer, w_in, rpb, w_mem_kv, w_br_a, w_br_b, w_br_c, w_br_m, w_gate, b_gate, w_out,
           norm_g):
    t = b * s
    h = rmsnorm(x, norm_g, BF16)
    z = matmul(h, w_in, layer, tm=2048, tn=NA_W, name="in_proj").reshape(b, s, -1)
    kv_m = matmul(mem_n, w_mem_kv, layer, tm=mem_n.shape[0], tn=MEM_W, name="mem_kv_proj")
    y_a = neighbourhood_attention(z, rpb).reshape(t, NA_W)
    y_b = fourier_mix(z).reshape(t, FN_W)
    y_c = dilated_attention(z)
    y_m = memory_attention(z, kv_m.reshape(b, -1, 2 * MEM_W)).reshape(t, MEM_W)
    merged = gated_merge(h, (y_a, y_b, y_c, y_m), w_gate, b_gate, (w_br_a, w_br_b, w_br_c, w_br_m),
                         (layer,) * 5)
    return matmul(merged, w_out, layer, tm=2048, tn=512, out_dtype=F32, residual=x, name="out_proj")


def kernel(x, mem, norm_mix, w_in, rpb, norm_mem, w_mem_kv, w_br_a, w_br_b, w_br_c, w_br_m, w_gate, b_gate,
           w_out, norm_ffn, dense_w_gate, dense_w_up, dense_w_down, router, exp_w_gate, exp_w_up, exp_w_down,
           final_norm):
    b, s, d = x.shape
    depth = norm_mix.shape[0]
    t = b * s
    xf = x.reshape(t, d)
    memf = mem.reshape(-1, d)
    normed = False
    for layer in range(depth):
        mem_n = rmsnorm(memf, norm_mem[layer], BF16)
        xf = _mixer(xf, mem_n, b, s, layer, w_in, rpb[layer], w_mem_kv, w_br_a, w_br_b, w_br_c, w_br_m, w_gate,
                    b_gate, w_out, norm_mix[layer])
        i = layer // 2
        if layer % 2 == 0:
            u = swiglu_up(xf, norm_ffn[layer], dense_w_gate, dense_w_up, i)
            xf = matmul(u, dense_w_down, i, tm=1024, tn=256, out_dtype=F32, residual=xf, name="swiglu_down")
        else:
            normed = layer == depth - 1
            xf = moe_layer(xf, norm_ffn[layer], router[i], exp_w_gate, exp_w_up, exp_w_down, i,
                           final_norm if normed else None)
    if not normed:
        xf = rmsnorm(xf, final_norm, F32)
    return xf.reshape(b, s, d)
```

```python
import functools

import numpy as np
import jax
import jax.numpy as jnp
from jax import lax
from jax.experimental import pallas as pl
from jax.experimental.pallas import tpu as pltpu

F32 = jnp.float32
BF16 = jnp.bfloat16

D_MODEL = 2048
GRID_W = 64
HEAD_DIM = 128
RMS_EPS = 1e-6
NA_HEADS = 6
NA_KH = 8
NA_KW = 16
NA_W = NA_HEADS * HEAD_DIM
FN_GROUPS = 6
FN_GROUP_W = 128
FN_W = FN_GROUPS * FN_GROUP_W
DIL_DILATIONS = (1, 4, 16)
DIL_HALF_WINDOW = 64
DIL_HEADS_PER_GROUP = 2
DIL_GROUP_W = DIL_HEADS_PER_GROUP * HEAD_DIM
Q_BLOCK = 128
ROT_DIM = HEAD_DIM // 4
ROPE_THETA = 500000.0
MEM_HEADS = 4
MEM_HEAD_DIM = 192
MEM_W = MEM_HEADS * MEM_HEAD_DIM
MEM_WINDOW = 256
N_EXPERTS = 8
TOP_K = 2
MOE_ROW_QUANTUM = 128
MASK_VALUE = -1e30

V7X_VMEM_BYTES = 64 * 1024 * 1024
VMEM_LIMIT_BYTES = V7X_VMEM_BYTES - 4 * 1024 * 1024
LANES = 128
N_DMA_PRIORITIES = 2


def _params(n_axes):
    return pltpu.CompilerParams(dimension_semantics=("arbitrary",) * n_axes,
                                vmem_limit_bytes=VMEM_LIMIT_BYTES)


def _dot(a, b):
    return jnp.dot(a, b, preferred_element_type=F32)


def _dot_nt(a, b):
    return lax.dot_general(a, b, (((1,), (1,)), ((), ())), preferred_element_type=F32)


def _softmax_pv(s, v):
    d = v.shape[1]
    top = jnp.max(s, axis=-1, keepdims=True)
    p = jnp.exp((s - top).astype(BF16))
    pv = _dot(p, jnp.concatenate([v, jnp.ones_like(v)], axis=1))
    num, den = pv[:, :d], pv[:, d:]
    return num / den, top + jnp.log(den)


def _rmsnorm_body(x, g):
    return x * lax.rsqrt(jnp.mean(x * x, axis=-1, keepdims=True) + RMS_EPS) * g


def _rmsnorm_kernel(x_ref, g_ref, o_ref):
    o_ref[...] = _rmsnorm_body(x_ref[...], g_ref[...]).astype(o_ref.dtype)


def rmsnorm(x, g, out_dtype, tm=512):
    m, d = x.shape
    return pl.pallas_call(
        _rmsnorm_kernel,
        grid=(m // tm,),
        in_specs=[pl.BlockSpec((tm, d), lambda i: (i, 0)),
                  pl.BlockSpec((1, d), lambda i: (0, 0))],
        out_specs=pl.BlockSpec((tm, d), lambda i: (i, 0)),
        out_shape=jax.ShapeDtypeStruct((m, d), out_dtype),
        compiler_params=_params(1),
        name="rmsnorm",
    )(x, g.reshape(1, d))


def _mm_kernel(a_ref, w_ref, o_ref):
    o_ref[...] = _dot(a_ref[...], w_ref[...].astype(BF16)).astype(o_ref.dtype)


def _mm_res_kernel(a_ref, w_ref, r_ref, o_ref):
    o_ref[...] = (r_ref[...] + _dot(a_ref[...], w_ref[...].astype(BF16))).astype(o_ref.dtype)


def matmul(a, w, layer, *, tm, tn, n_cols=None, out_dtype=BF16, residual=None, name="matmul"):
    m, k = a.shape
    n = w.shape[2] if n_cols is None else n_cols
    in_specs = [pl.BlockSpec((tm, k), lambda i, j: (i, 0)),
                pl.BlockSpec((None, k, tn), lambda i, j: (layer, 0, j))]
    args = [a, w]
    kernel = _mm_kernel
    if residual is not None:
        in_specs.append(pl.BlockSpec((tm, tn), lambda i, j: (i, j)))
        args.append(residual)
        kernel = _mm_res_kernel
    return pl.pallas_call(
        kernel,
        grid=(m // tm, n // tn),
        in_specs=in_specs,
        out_specs=pl.BlockSpec((tm, tn), lambda i, j: (i, j)),
        out_shape=jax.ShapeDtypeStruct((m, n), out_dtype),
        compiler_params=_params(2),
        name=name,
    )(*args)


NA_Q_ROWS = 4
NA_KEY_ROWS = NA_KH + NA_Q_ROWS


def _na_key_row_start(step, rows):
    return np.clip(step * NA_Q_ROWS - NA_KH // 2, 0, rows - NA_KEY_ROWS)


def _na_bias_table(rpb, rows):
    col = np.arange(GRID_W)
    col_start = np.clip(col - NA_KW // 2, 0, GRID_W - NA_KW)
    rel_col = col[None, :] - col[:, None] + (NA_KW - 1)
    col_ok = (col[None, :] >= col_start[:, None]) & (col[None, :] < col_start[:, None] + NA_KW)
    pick = (np.clip(rel_col, 0, 2 * NA_KW - 2)[None] == np.arange(2 * NA_KW - 1)[:, None, None]).astype(np.float32)
    slab = jnp.einsum("hdr,rcw->hdcw", rpb.astype(F32), pick, precision=lax.Precision.HIGHEST)
    n_steps = rows // NA_Q_ROWS
    rel_row = np.zeros((n_steps, NA_Q_ROWS, NA_KEY_ROWS), np.int64)
    row_ok = np.zeros((n_steps, NA_Q_ROWS, NA_KEY_ROWS), bool)
    for step in range(n_steps):
        for i in range(NA_Q_ROWS):
            r = step * NA_Q_ROWS + i
            win = np.clip(r - NA_KH // 2, 0, rows - NA_KH)
            key_row = _na_key_row_start(step, rows) + np.arange(NA_KEY_ROWS)
            row_ok[step, i] = (key_row >= win) & (key_row < win + NA_KH)
            rel_row[step, i] = key_row - r + (NA_KH - 1)
    assert all((rel_row[s_] * row_ok[s_] == rel_row[1] * row_ok[1]).all() and (row_ok[s_] == row_ok[1]).all()
               for s_ in range(1, n_steps - 1))
    slab = jnp.where(col_ok, slab, MASK_VALUE)
    hidden = jnp.full(slab.shape[:1] + slab.shape[2:], MASK_VALUE, F32)
    cases = []
    for step in (0, 1, n_steps - 1):
        q_rows = [jnp.concatenate([slab[:, rel_row[step, i, k]] if row_ok[step, i, k] else hidden
                                   for k in range(NA_KEY_ROWS)], axis=-1) for i in range(NA_Q_ROWS)]
        cases.append(jnp.concatenate(q_rows, axis=-2))
    return jnp.stack(cases)


def _na_kernel(q_ref, k_ref, v_ref, b_ref, o_ref, *, rows):
    step = pl.program_id(1)
    key_row0 = jnp.clip(step * NA_Q_ROWS - NA_KH // 2, 0, rows - NA_KEY_ROWS)
    k0 = pl.multiple_of(key_row0 * GRID_W, GRID_W)
    n_keys = NA_KEY_ROWS * GRID_W
    scale = HEAD_DIM ** -0.5
    for h in range(NA_HEADS):
        cols = slice(h * HEAD_DIM, (h + 1) * HEAD_DIM)
        k = k_ref[0, pl.ds(k0, n_keys), cols]
        v = v_ref[0, pl.ds(k0, n_keys), cols]
        s = _dot_nt(q_ref[0, :, cols], k) * scale + b_ref[0, h]
        o_ref[0, :, cols] = _softmax_pv(s, v)[0].astype(o_ref.dtype)


def neighbourhood_attention(z, rpb):
    b, s, _ = z.shape
    rows = s // GRID_W
    n_steps = rows // NA_Q_ROWS
    bias = _na_bias_table(rpb, rows)
    tq = NA_Q_ROWS * GRID_W
    bias_case = lambda i, r: (jnp.where(r == 0, 0, jnp.where(r == n_steps - 1, 2, 1)), 0, 0, 0)
    return pl.pallas_call(
        functools.partial(_na_kernel, rows=rows),
        grid=(b, n_steps),
        in_specs=[pl.BlockSpec((1, tq, NA_W), lambda i, r: (i, r, 0)),
                  pl.BlockSpec((1, s, NA_W), lambda i, r: (i, 0, 1)),
                  pl.BlockSpec((1, s, NA_W), lambda i, r: (i, 0, 2)),
                  pl.BlockSpec((1,) + bias.shape[1:], bias_case)],
        out_specs=pl.BlockSpec((1, tq, NA_W), lambda i, r: (i, r, 0)),
        out_shape=jax.ShapeDtypeStruct((b, s, NA_W), BF16),
        compiler_params=_params(2),
        name="neighbourhood_attention",
    )(z, z, z, bias)


def _dft_tables(s):
    half = s // 2
    n = np.arange(half, dtype=np.int64)
    ang = 2.0 * np.pi * ((n[:, None] * n[None, :]) % half) / half
    pos = np.concatenate([np.cos(ang), -np.sin(ang)], axis=1)
    c = np.arange(FN_GROUP_W, dtype=np.int64)
    ang_c = 2.0 * np.pi * ((c[:, None] * c[None, :]) % FN_GROUP_W) / FN_GROUP_W
    chan = np.concatenate([np.cos(ang_c), np.sin(ang_c)], axis=1)
    tw = np.pi * n[:, None] / half * np.ones((1, FN_GROUP_W))
    return (jnp.asarray(pos, F32).astype(BF16), jnp.asarray(chan, F32).astype(BF16),
            jnp.asarray(np.cos(tw), F32), jnp.asarray(np.sin(tw), F32))


def _fourier_kernel(u_ref, chan_ref, twc_ref, tws_ref, pos_ref, o_ref, ve_scr, vo_scr, y_scr, *, s, norm):
    half = s // 2

    @pl.when(pl.program_id(1) == 0)
    def _():
        twc, tws = twc_ref[...], tws_ref[...]
        for g in range(FN_GROUPS):
            cols = slice(g * FN_GROUP_W, (g + 1) * FN_GROUP_W)
            t = _dot(u_ref[0, :, cols], chan_ref[...])
            top, bot = t[:half], t[half:]
            fold = top + bot
            diff = top - bot
            dc, ds = diff[:, :FN_GROUP_W], diff[:, FN_GROUP_W:]
            ve_scr[0:half, cols] = fold[:, :FN_GROUP_W].astype(BF16)
            ve_scr[half:s, cols] = fold[:, FN_GROUP_W:].astype(BF16)
            vo_scr[0:half, cols] = (twc * dc - tws * ds).astype(BF16)
            vo_scr[half:s, cols] = (tws * dc + twc * ds).astype(BF16)

    rows = pos_ref.shape[0]
    y_even = _dot(pos_ref[...], ve_scr[...]) * norm
    y_odd = _dot(pos_ref[...], vo_scr[...]) * norm
    for g in range(FN_GROUPS):
        cols = slice(g * FN_GROUP_W, (g + 1) * FN_GROUP_W)
        y_scr[g, pl.ds(0, rows, stride=2), :] = y_even[:, cols]
        y_scr[g, pl.ds(1, rows, stride=2), :] = y_odd[:, cols]
        o_ref[0, :, cols] = y_scr[g].astype(o_ref.dtype)


def fourier_mix(z, ts=512):
    b, s, _ = z.shape
    pos_tab, chan_tab, tw_cos, tw_sin = _dft_tables(s)
    norm = float((s * FN_GROUP_W) ** -0.5)
    whole = lambda a: pl.BlockSpec(a.shape, lambda i, j: (0, 0))
    return pl.pallas_call(
        functools.partial(_fourier_kernel, s=s, norm=norm),
        grid=(b, s // (2 * ts)),
        in_specs=[pl.BlockSpec((1, s, FN_W), lambda i, j: (i, 0, 6)),
                  whole(chan_tab), whole(tw_cos), whole(tw_sin),
                  pl.BlockSpec((ts, s), lambda i, j: (j, 0))],
        out_specs=pl.BlockSpec((1, 2 * ts, FN_W), lambda i, j: (i, j, 0)),
        out_shape=jax.ShapeDtypeStruct((b, s, FN_W), BF16),
        scratch_shapes=[pltpu.VMEM((s, FN_W), BF16), pltpu.VMEM((s, FN_W), BF16),
                        pltpu.VMEM((FN_GROUPS, 2 * ts, FN_GROUP_W), F32)],
        compiler_params=_params(2),
        name="fourier_mix",
    )(z, chan_tab, tw_cos, tw_sin, pos_tab)


def _rope_tables(s):
    half = ROT_DIM // 2
    inv_freq = ROPE_THETA ** (-2.0 * jnp.arange(half, dtype=F32) / ROT_DIM)
    ang = jnp.arange(s, dtype=jnp.int32).astype(F32)[:, None] * inv_freq[None, :]
    cos, sin = jnp.cos(ang), jnp.sin(ang)
    ones = jnp.ones((s, HEAD_DIM - ROT_DIM), F32)
    zeros_half = jnp.zeros((s, half), F32)
    zeros_rest = jnp.zeros((s, HEAD_DIM - ROT_DIM), F32)
    cos_t = jnp.concatenate([cos, cos, ones], axis=1)
    sin_lo = jnp.concatenate([-sin, zeros_half, zeros_rest], axis=1)
    sin_hi = jnp.concatenate([zeros_half, sin, zeros_rest], axis=1)
    return cos_t, sin_lo, sin_hi


def _dilated_kernel(q0, q1, q2, k0, k1, k2, v0, v1, v2, cos_ref, slo_ref, shi_ref, y_ref,
                    rq_ref, rk_ref, rv_ref, o_scr, l_scr, *, seq):
    half = ROT_DIM // 2
    scale = HEAD_DIM ** -0.5
    chunk = 4 * Q_BLOCK
    for g, (q_ref, k_ref, v_ref) in enumerate(((q0, k0, v0), (q1, k1, v1), (q2, k2, v2))):
        for c0 in range(0, seq, chunk):
            rows = slice(c0, c0 + chunk)
            cs, lo, hi = cos_ref[rows, :], slo_ref[rows, :], shi_ref[rows, :]
            for src, dst in ((q_ref, rq_ref), (k_ref, rk_ref)):
                x = src[0, rows, :].astype(F32)
                dst[g, rows, :] = x * cs + pltpu.roll(x, HEAD_DIM - half, 1) * lo + pltpu.roll(x, half, 1) * hi
            rv_ref[g, rows, :] = v_ref[0, rows, :].astype(F32)

    masks = {}

    def window_mask(n_keys, offset):
        if (n_keys, offset) not in masks:
            rel = (lax.broadcasted_iota(jnp.int32, (Q_BLOCK, n_keys), 1)
                   - lax.broadcasted_iota(jnp.int32, (Q_BLOCK, n_keys), 0) + offset)
            masks[(n_keys, offset)] = jnp.abs(rel) <= DIL_HALF_WINDOW
        return masks[(n_keys, offset)]

    for g, d in enumerate(DIL_DILATIONS):
        length = seq // d
        n_keys = min(length, Q_BLOCK + 2 * DIL_HALF_WINDOW)
        for rho in range(d):
            for m_q in range(0, length, Q_BLOCK):
                m_k = min(max(m_q - DIL_HALF_WINDOW, 0), length - n_keys)
                q_rows = pl.ds(rho + d * m_q, Q_BLOCK, stride=d) if d > 1 else pl.ds(m_q, Q_BLOCK)
                k_rows = pl.ds(rho + d * m_k, n_keys, stride=d) if d > 1 else pl.ds(m_k, n_keys)
                q = rq_ref[g, q_rows, :].astype(BF16)
                k = rk_ref[g, k_rows, :].astype(BF16)
                v = rv_ref[g, k_rows, :].astype(BF16)
                s = jnp.where(window_mask(n_keys, m_k - m_q), _dot_nt(q, k) * scale, MASK_VALUE)
                o_scr[g, q_rows, :], l_scr[g, q_rows, :] = _softmax_pv(s, v)

    n_groups = len(DIL_DILATIONS)
    for c0 in range(0, seq, chunk):
        rows = slice(c0, c0 + chunk)
        lses = [l_scr[g, rows, :] for g in range(n_groups)]
        top = functools.reduce(jnp.maximum, lses)
        ws = [jnp.exp(l - top) for l in lses]
        num = sum(w * o_scr[g, rows, :] for g, w in enumerate(ws))
        y_ref[0, rows, :] = (num / sum(ws)).astype(y_ref.dtype)


def dilated_attention(z):
    b, s, _ = z.shape
    n_groups = len(DIL_DILATIONS)
    tables = _rope_tables(s)
    heads_per_block = NA_W // HEAD_DIM

    def head_spec(t, g):
        first = (3 + t) * heads_per_block + g * DIL_HEADS_PER_GROUP
        return pl.BlockSpec((1, s, HEAD_DIM), lambda i, h: (i, 0, first + h))

    table_spec = pl.BlockSpec((s, HEAD_DIM), lambda i, h: (0, 0))
    scratch = pltpu.VMEM((n_groups, s, HEAD_DIM), F32)
    return pl.pallas_call(
        functools.partial(_dilated_kernel, seq=s),
        grid=(b, DIL_HEADS_PER_GROUP),
        in_specs=[head_spec(t, g) for t in range(3) for g in range(n_groups)] + [table_spec] * 3,
        out_specs=pl.BlockSpec((1, s, HEAD_DIM), lambda i, h: (i, 0, h)),
        out_shape=jax.ShapeDtypeStruct((b, s, DIL_GROUP_W), BF16),
        scratch_shapes=[scratch] * 5,
        compiler_params=_params(2),
        name="dilated_attention",
    )(*([z] * 9), *tables).reshape(b * s, DIL_GROUP_W)


def _mem_attn_kernel(q_ref, k_ref, v_ref, o_ref):
    scale = MEM_HEAD_DIM ** -0.5
    col = lax.broadcasted_iota(jnp.int32, (1, MEM_WINDOW), 1)
    parts = []
    for h in range(MEM_HEADS):
        start = h * MEM_HEAD_DIM // LANES * LANES
        win = slice(start, start + MEM_WINDOW)
        in_head = (col >= h * MEM_HEAD_DIM - start) & (col < (h + 1) * MEM_HEAD_DIM - start)
        q = q_ref[0, :, win]
        s = _dot_nt(jnp.where(in_head, q, jnp.zeros_like(q)), k_ref[0, :, win]) * scale
        o = jnp.where(in_head, _softmax_pv(s, v_ref[0, :, win])[0], 0.0)
        pads = (start, MEM_W - start - MEM_WINDOW)
        left, right = (jnp.zeros((o.shape[0], n), F32) for n in pads)
        parts.append(jnp.concatenate([a for a in (left, o, right) if a.shape[1]], axis=1))
    o_ref[0] = sum(parts).astype(o_ref.dtype)


def memory_attention(z, kv, tq=512):
    b, s, _ = z.shape
    m = kv.shape[1]
    return pl.pallas_call(
        _mem_attn_kernel,
        grid=(b, s // tq),
        in_specs=[pl.BlockSpec((1, tq, MEM_W), lambda i, j: (i, j, 7)),
                  pl.BlockSpec((1, m, MEM_W), lambda i, j: (i, 0, 0)),
                  pl.BlockSpec((1, m, MEM_W), lambda i, j: (i, 0, 1))],
        out_specs=pl.BlockSpec((1, tq, MEM_W), lambda i, j: (i, j, 0)),
        out_shape=jax.ShapeDtypeStruct((b, s, MEM_W), BF16),
        compiler_params=_params(2),
        name="memory_attention",
    )(z, kv, kv)


def _merge_kernel(h_ref, ya_ref, yb_ref, yc_ref, ym_ref, wg0, wg1, wg2, wg3, bg0, bg1, bg2, bg3,
                  wa_ref, wb_ref, wc_ref, wm_ref, o_ref):
    h = h_ref[...]
    acc = None
    for y_ref, wg, bg, wbr in ((ya_ref, wg0, bg0, wa_ref), (yb_ref, wg1, bg1, wb_ref),
                               (yc_ref, wg2, bg2, wc_ref), (ym_ref, wg3, bg3, wm_ref)):
        gate = jax.nn.sigmoid(_dot(h, wg[...].astype(BF16)) + bg[...])
        term = gate * _dot(y_ref[...], wbr[...].astype(BF16))
        acc = term if acc is None else acc + term
    o_ref[...] = acc.astype(o_ref.dtype)


def _gate_block(i, j, *, layer, branch, n_tiles):
    return (layer, 0, branch * n_tiles + j)


def gated_merge(h, ys, w_gate, b_gate, w_brs, layers, tm=1024, tn=256):
    m, d = h.shape
    n_tiles = d // tn
    row = lambda width: pl.BlockSpec((tm, width), lambda i, j: (i, 0))
    gate_maps = [functools.partial(_gate_block, layer=layers[0], branch=br, n_tiles=n_tiles) for br in range(4)]
    gate_w = [pl.BlockSpec((None, d, tn), gm) for gm in gate_maps]
    gate_b = [pl.BlockSpec((None, 1, tn), gm) for gm in gate_maps]
    br_w = [pl.BlockSpec((None, w.shape[1], tn), functools.partial(lambda i, j, l: (l, 0, j), l=l))
            for w, l in zip(w_brs, layers[1:])]
    b_gate3 = b_gate.reshape(b_gate.shape[0], 1, -1)
    return pl.pallas_call(
        _merge_kernel,
        grid=(m // tm, n_tiles),
        in_specs=[row(d)] + [row(y.shape[1]) for y in ys] + gate_w + gate_b + br_w,
        out_specs=pl.BlockSpec((tm, tn), lambda i, j: (i, j)),
        out_shape=jax.ShapeDtypeStruct((m, d), BF16),
        compiler_params=_params(2),
        name="gated_merge",
    )(h, *ys, w_gate, w_gate, w_gate, w_gate, b_gate3, b_gate3, b_gate3, b_gate3, *w_brs)


def _silu_mul(g, u):
    return g * jax.nn.sigmoid(g) * u


def _swiglu_up_kernel(x_ref, g_ref, wg_ref, wu_ref, o_ref, h_scr):
    @pl.when(pl.program_id(1) == 0)
    def _():
        h_scr[...] = _rmsnorm_body(x_ref[...], g_ref[...]).astype(h_scr.dtype)

    a = h_scr[...]
    o_ref[...] = _silu_mul(_dot(a, wg_ref[...].astype(BF16)), _dot(a, wu_ref[...].astype(BF16))).astype(o_ref.dtype)


def swiglu_up(x, norm_g, wg, wu, layer, tm=1024, tf=512):
    m, k = x.shape
    f = wg.shape[2]
    return pl.pallas_call(
        _swiglu_up_kernel,
        grid=(m // tm, f // tf),
        in_specs=[pl.BlockSpec((tm, k), lambda i, j: (i, 0)),
                  pl.BlockSpec((1, k), lambda i, j: (0, 0)),
                  pl.BlockSpec((None, k, tf), lambda i, j: (layer, 0, j)),
                  pl.BlockSpec((None, k, tf), lambda i, j: (layer, 0, j))],
        out_specs=pl.BlockSpec((tm, tf), lambda i, j: (i, j)),
        out_shape=jax.ShapeDtypeStruct((m, f), BF16),
        scratch_shapes=[pltpu.VMEM((tm, k), BF16)],
        compiler_params=_params(2),
        name="swiglu_up",
    )(x, norm_g.reshape(1, k), wg, wu)


def _router_kernel(x_ref, g_ref, r_ref, idx_ref, rank_ref, gate_ref, cnt_ref):
    @pl.when(pl.program_id(0) == 0)
    def _():
        cnt_ref[...] = jnp.zeros_like(cnt_ref)

    h = _rmsnorm_body(x_ref[...], g_ref[...])
    r = r_ref[...]
    h_hi, r_hi = h.astype(BF16), r.astype(BF16)
    h_lo, r_lo = (h - h_hi.astype(F32)).astype(BF16), (r - r_hi.astype(F32)).astype(BF16)
    logits = _dot(h_hi, r_hi) + (_dot(h_hi, r_lo) + _dot(h_lo, r_hi))
    lane = lax.broadcasted_iota(jnp.int32, logits.shape, 1).astype(F32)
    neg = -jnp.inf
    lg = jnp.where(lane < N_EXPERTS, logits, neg)
    v1 = jnp.max(lg, axis=-1, keepdims=True)
    i1 = jnp.min(jnp.where(lg == v1, lane, float(LANES)), axis=-1, keepdims=True)
    lg2 = jnp.where(lane == i1, neg, lg)
    v2 = jnp.max(lg2, axis=-1, keepdims=True)
    i2 = jnp.min(jnp.where(lg2 == v2, lane, float(LANES)), axis=-1, keepdims=True)
    e = jnp.exp(v2 - v1)
    g1 = 1.0 / (1.0 + e)
    g2 = e / (1.0 + e)
    onehot = jnp.where((lane == i1) | (lane == i2), 1.0, 0.0)
    tt = onehot.shape[0]
    earlier = lax.broadcasted_iota(jnp.int32, (tt, tt), 1) < lax.broadcasted_iota(jnp.int32, (tt, tt), 0)
    pos = _dot(jnp.where(earlier, 1.0, 0.0).astype(BF16), onehot.astype(BF16)) + cnt_ref[...]
    r1 = jnp.sum(jnp.where(lane == i1, pos, 0.0), axis=-1, keepdims=True)
    r2 = jnp.sum(jnp.where(lane == i2, pos, 0.0), axis=-1, keepdims=True)
    cnt_ref[...] += jnp.sum(onehot, axis=0, keepdims=True)
    idx_ref[...] = jnp.where(lane == 0, i1, jnp.where(lane == 1, i2, 0.0)).astype(jnp.int32)
    rank_ref[...] = jnp.where(lane == 0, r1, jnp.where(lane == 1, r2, 0.0)).astype(jnp.int32)
    gate_ref[...] = jnp.where(lane == 0, g1, jnp.where(lane == 1, g2, 0.0))


def route(x, g, router, tm=512):
    m, d = x.shape
    r_pad = jnp.pad(router, ((0, 0), (0, LANES - N_EXPERTS)))
    lane_spec = pl.BlockSpec((tm, LANES), lambda i: (i, 0))
    return pl.pallas_call(
        _router_kernel,
        grid=(m // tm,),
        in_specs=[pl.BlockSpec((tm, d), lambda i: (i, 0)),
                  pl.BlockSpec((1, d), lambda i: (0, 0)),
                  pl.BlockSpec((d, LANES), lambda i: (0, 0))],
        out_specs=[lane_spec, lane_spec, lane_spec, pl.BlockSpec((1, LANES), lambda i: (0, 0))],
        out_shape=[jax.ShapeDtypeStruct((m, LANES), jnp.int32),
                   jax.ShapeDtypeStruct((m, LANES), jnp.int32),
                   jax.ShapeDtypeStruct((m, LANES), F32),
                   jax.ShapeDtypeStruct((1, LANES), F32)],
        compiler_params=_params(1),
        name="route",
    )(x, g.reshape(1, d), r_pad)


def _dispatch_kernel(dest_ref, lo_ref, hi_ref, na_ref, x_hbm, g_ref, a_ref, src_ref, rows_ref, sems):
    j = pl.program_id(0)
    tm = rows_ref.shape[1]
    n_active = na_ref[0]

    def row_copy(tile, r):
        slot = lax.rem(tile, 2)
        return pltpu.make_async_copy(x_hbm.at[pl.ds(src_ref[tile * tm + r], 1)],
                                     rows_ref.at[slot, pl.ds(r, 1)], sems.at[slot])

    def start_tile(tile):
        def start(r2, c):
            for u in range(N_DMA_PRIORITIES):
                row_copy(tile, N_DMA_PRIORITIES * r2 + u).start(priority=u)
            return c

        lax.fori_loop(0, tm // N_DMA_PRIORITIES, start, 0, unroll=4)

    @pl.when(j == 0)
    def _():
        def clear(i, c):
            src_ref[i] = 0
            return c

        for e in range(N_EXPERTS):
            lax.fori_loop(lo_ref[e], hi_ref[e], clear, 0)

        def fill(i, c):
            src_ref[dest_ref[i]] = lax.shift_right_logical(i, 1)
            return c

        lax.fori_loop(0, dest_ref.shape[0], fill, 0, unroll=8)

        @pl.when(n_active > 0)
        def _():
            start_tile(0)

    @pl.when(j + 1 < n_active)
    def _():
        start_tile(j + 1)

    @pl.when(j < n_active)
    def _():
        def wait(r, c):
            row_copy(j, r).wait()
            return c

        lax.fori_loop(0, tm, wait, 0, unroll=8)
        a_ref[...] = _rmsnorm_body(rows_ref[lax.rem(j, 2)], g_ref[...]).astype(a_ref.dtype)

    @pl.when(j >= n_active)
    def _():
        a_ref[...] = jnp.zeros_like(a_ref)


def _combine_kernel(dest_ref, x_ref, gate_ref, g_ref, y_hbm, o_ref, rows_ref, sems, *, apply_norm):
    tt = x_ref.shape[0]
    i = pl.program_id(0)

    def row_copy(tile, r, k):
        slot = lax.rem(tile, 2)
        row = dest_ref[TOP_K * (tile * tt + r) + k]
        return pltpu.make_async_copy(y_hbm.at[pl.ds(row, 1)], rows_ref.at[slot, k, pl.ds(r, 1)], sems.at[slot])

    def start_tile(tile):
        def start(r, c):
            for k in range(TOP_K):
                row_copy(tile, r, k).start(priority=k % N_DMA_PRIORITIES)
            return c

        lax.fori_loop(0, tt, start, 0, unroll=4)

    @pl.when(i == 0)
    def _():
        start_tile(0)

    @pl.when(i + 1 < pl.num_programs(0))
    def _():
        start_tile(i + 1)

    def wait(r, c):
        for k in range(TOP_K):
            row_copy(i, r, k).wait()
        return c

    lax.fori_loop(0, tt, wait, 0, unroll=4)
    gates = gate_ref[...]
    slot = lax.rem(i, 2)
    y = x_ref[...]
    for k in range(TOP_K):
        y = y + gates[:, k:k + 1] * rows_ref[slot, k]
    o_ref[...] = _rmsnorm_body(y, g_ref[...]) if apply_norm else y


def _on_valid_rows(n_valid, o_ref, compute):
    tm = o_ref.shape[0]
    for n in range(MOE_ROW_QUANTUM, tm + 1, MOE_ROW_QUANTUM):
        @pl.when((n_valid > n - MOE_ROW_QUANTUM) & (n_valid <= n))
        def _(n=n):
            o_ref[:n, :] = compute(n).astype(o_ref.dtype)
            if n < tm:
                o_ref[n:, :] = jnp.zeros((tm - n, o_ref.shape[1]), o_ref.dtype)

    @pl.when(n_valid == 0)
    def _():
        o_ref[...] = jnp.zeros_like(o_ref)


def _stream_expert_weights(meta, w_hbms, w_scr, sems, layer, width):
    te_ref, first_ref, group_ref, next_ref, n_groups_ref = meta
    c, j = pl.program_id(0), pl.program_id(1)
    group = group_ref[j]
    n_groups = n_groups_ref[0]
    block = c * n_groups + group
    slot = lax.rem(block, 2)

    def copies(expert, col, to_slot):
        cols = pl.ds(pl.multiple_of(col * width, width), width)
        return [pltpu.make_async_copy(w.at[layer, expert, :, cols], w_scr.at[to_slot, i], sems.at[to_slot, i])
                for i, w in enumerate(w_hbms)]

    @pl.when(first_ref[j] == 1)
    def _():
        @pl.when(block == 0)
        def _():
            for cp in copies(te_ref[j], c, slot):
                cp.start()

        for cp in copies(te_ref[j], c, slot):
            cp.wait()
        more_groups = group + 1 < n_groups

        @pl.when(more_groups)
        def _():
            for cp in copies(next_ref[j], c, 1 - slot):
                cp.start()

        @pl.when(jnp.logical_not(more_groups) & (c + 1 < pl.num_programs(0)))
        def _():
            for cp in copies(te_ref[0], c + 1, 1 - slot):
                cp.start()

    return slot


def _moe_up_kernel(te_ref, na_ref, nv_ref, first_ref, group_ref, next_ref, ng_ref, a_ref, wg_hbm, wu_hbm, o_ref,
                   w_scr, sems, *, layer):
    meta = (te_ref, first_ref, group_ref, next_ref, ng_ref)
    slot = _stream_expert_weights(meta, (wg_hbm, wu_hbm), w_scr, sems, layer, o_ref.shape[1])

    def compute(n):
        a = a_ref[:n, :]
        return _silu_mul(_dot(a, w_scr[slot, 0].astype(BF16)), _dot(a, w_scr[slot, 1].astype(BF16)))

    _on_valid_rows(nv_ref[pl.program_id(1)], o_ref, compute)


def _moe_down_kernel(te_ref, na_ref, nv_ref, first_ref, group_ref, next_ref, ng_ref, a_ref, w_hbm, o_ref,
                     w_scr, sems, *, layer, tk):
    meta = (te_ref, first_ref, group_ref, next_ref, ng_ref)
    slot = _stream_expert_weights(meta, (w_hbm,), w_scr, sems, layer, o_ref.shape[1])

    def compute(n):
        acc = None
        for c in range(a_ref.shape[1] // tk):
            part = _dot(a_ref[:n, c * tk:(c + 1) * tk], w_scr[slot, 0, c * tk:(c + 1) * tk, :].astype(BF16))
            acc = part if acc is None else acc + part
        return acc

    _on_valid_rows(nv_ref[pl.program_id(1)], o_ref, compute)


def moe_layer(x, norm_g, router, wg, wu, wd, layer, out_g, tm=512, tf=1024, tn=512, tk=1024, tt=256):
    assert TOP_K == 2
    t, d = x.shape
    _, n_exp, _, f = wg.shape
    n_rows = t * TOP_K
    n_tiles = n_rows // tm + n_exp
    idx, rank, gate, cnt = route(x, norm_g, router)
    counts = cnt[0, :n_exp].astype(jnp.int32)
    tiles_per = (counts + tm - 1) // tm
    tile_end = jnp.cumsum(tiles_per)
    start = (tile_end - tiles_per) * tm
    n_active = tile_end[-1:]
    tile_ids = jnp.arange(n_tiles, dtype=jnp.int32)
    tile_expert = jnp.sum((tile_ids[:, None] >= tile_end[None, :]).astype(jnp.int32), axis=1)
    last_expert = jnp.sum((n_active - 1 >= tile_end).astype(jnp.int32))
    tile_expert = jnp.where(tile_ids < n_active, tile_expert, last_expert).astype(jnp.int32)
    group_end = start + counts
    tile_valid = jnp.where(tile_ids < n_active, jnp.clip(group_end[tile_expert] - tile_ids * tm, 0, tm), 0)
    prev_expert = jnp.concatenate([jnp.full((1,), -1, jnp.int32), tile_expert[:-1]])
    tile_first = ((tile_ids < n_active) & (tile_expert != prev_expert)).astype(jnp.int32)
    tile_group = jnp.cumsum(tile_first) - 1
    n_groups = jnp.sum(tile_first, keepdims=True)
    group_expert = jnp.argsort(tiles_per == 0, stable=True).astype(jnp.int32)
    tile_next = group_expert[jnp.minimum(tile_group + 1, n_exp - 1)]
    tile_meta = (tile_expert, n_active, tile_valid, tile_first, tile_group, tile_next, n_groups)
    picked = idx[:, :TOP_K, None] == jnp.arange(n_exp, dtype=jnp.int32)
    dest = (jnp.sum(jnp.where(picked, start, 0), axis=-1) + rank[:, :TOP_K]).reshape(n_rows)

    row_tile = lambda j, na: jnp.maximum(jnp.minimum(j, na[0] - 1), 0)
    a = pl.pallas_call(
        _dispatch_kernel,
        grid_spec=pltpu.PrefetchScalarGridSpec(
            num_scalar_prefetch=4,
            grid=(n_tiles,),
            in_specs=[pl.BlockSpec(memory_space=pl.ANY),
                      pl.BlockSpec((1, d), lambda j, dst, lo, hi, na: (0, 0))],
            out_specs=pl.BlockSpec((tm, d), lambda j, dst, lo, hi, na: (j, 0)),
            scratch_shapes=[pltpu.SMEM((n_tiles * tm,), jnp.int32),
                            pltpu.VMEM((2, tm, d), F32),
                            pltpu.SemaphoreType.DMA((2,))]),
        out_shape=jax.ShapeDtypeStruct((n_tiles * tm, d), BF16),
        compiler_params=_params(1),
        name="moe_dispatch",
    )(dest, group_end, tile_end * tm, n_active, x, norm_g.reshape(1, d))
    lhs_tile = lambda c, j, te, na, *_: (row_tile(j, na), 0)
    out_tile = lambda c, j, *_: (j, c)
    in_hbm = pl.BlockSpec(memory_space=pl.ANY)
    up = pl.pallas_call(
        functools.partial(_moe_up_kernel, layer=layer),
        grid_spec=pltpu.PrefetchScalarGridSpec(
            num_scalar_prefetch=len(tile_meta),
            grid=(f // tf, n_tiles),
            in_specs=[pl.BlockSpec((tm, d), lhs_tile), in_hbm, in_hbm],
            out_specs=pl.BlockSpec((tm, tf), out_tile),
            scratch_shapes=[pltpu.VMEM((2, 2, d, tf), F32), pltpu.SemaphoreType.DMA((2, 2))]),
        out_shape=jax.ShapeDtypeStruct((n_tiles * tm, f), BF16),
        compiler_params=_params(2),
        name="moe_up",
    )(*tile_meta, a, wg, wu)
    down = pl.pallas_call(
        functools.partial(_moe_down_kernel, layer=layer, tk=tk),
        grid_spec=pltpu.PrefetchScalarGridSpec(
            num_scalar_prefetch=len(tile_meta),
            grid=(d // tn, n_tiles),
            in_specs=[pl.BlockSpec((tm, f), lhs_tile), in_hbm],
            out_specs=pl.BlockSpec((tm, tn), out_tile),
            scratch_shapes=[pltpu.VMEM((2, 1, f, tn), F32), pltpu.SemaphoreType.DMA((2, 1))]),
        out_shape=jax.ShapeDtypeStruct((n_tiles * tm, d), F32),
        compiler_params=_params(2),
        name="moe_down",
    )(*tile_meta, up, wd)
    g_out = jnp.ones((d,), F32) if out_g is None else out_g
    return pl.pallas_call(
        functools.partial(_combine_kernel, apply_norm=out_g is not None),
        grid_spec=pltpu.PrefetchScalarGridSpec(
            num_scalar_prefetch=1,
            grid=(t // tt,),
            in_specs=[pl.BlockSpec((tt, d), lambda i, dst: (i, 0)),
                      pl.BlockSpec((tt, LANES), lambda i, dst: (i, 0)),
                      pl.BlockSpec((1, d), lambda i, dst: (0, 0)),
                      pl.BlockSpec(memory_space=pl.ANY)],
            out_specs=pl.BlockSpec((tt, d), lambda i, dst: (i, 0)),
            scratch_shapes=[pltpu.VMEM((2, TOP_K, tt, d), F32),
                            pltpu.SemaphoreType.DMA((2,))]),
        out_shape=jax.ShapeDtypeStruct((t, d), F32),
        compiler_params=_params(1),
        name="moe_combine",
    )(dest, x, gate, g_out.reshape(1, d), down)


def _mixer(x, mem_n, b, s, layer, w_in, rpb, w_mem_kv, w_br_a, w_br_b, w_br_c, w_br_m, w_gate, b_gate, w_out,
           norm_g):
    t = b * s
    h = rmsnorm(x, norm_g, BF16)
    z = matmul(h, w_in, layer, tm=2048, tn=NA_W, name="in_proj").reshape(b, s, -1)
    kv_m = matmul(mem_n, w_mem_kv, layer, tm=mem_n.shape[0], tn=MEM_W, name="mem_kv_proj")
    y_a = neighbourhood_attention(z, rpb).reshape(t, NA_W)
    y_b = fourier_mix(z).reshape(t, FN_W)
    y_c = dilated_attention(z)
    y_m = memory_attention(z, kv_m.reshape(b, -1, 2 * MEM_W)).reshape(t, MEM_W)
    merged = gated_merge(h, (y_a, y_b, y_c, y_m), w_gate, b_gate, (w_br_a, w_br_b, w_br_c, w_br_m),
                         (layer,) * 5)
    return matmul(merged, w_out, layer, tm=2048, tn=512, out_dtype=F32, residual=x, name="out_proj")


def kernel(x, mem, norm_mix, w_in, rpb, norm_mem, w_mem_kv, w_br_a, w_br_b, w_br_c, w_br_m, w_gate, b_gate,
           w_out, norm_ffn, dense_w_gate, dense_w_up, dense_w_down, router, exp_w_gate, exp_w_up, exp_w_down,
           final_norm):
    b, s, d = x.shape
    depth = norm_mix.shape[0]
    t = b * s
    xf = x.reshape(t, d)
    memf = mem.reshape(-1, d)
    normed = False
    for layer in range(depth):
        mem_n = rmsnorm(memf, norm_mem[layer], BF16)
        xf = _mixer(xf, mem_n, b, s, layer, w_in, rpb[layer], w_mem_kv, w_br_a, w_br_b, w_br_c, w_br_m, w_gate,
                    b_gate, w_out, norm_mix[layer])
        i = layer // 2
        if layer % 2 == 0:
            u = swiglu_up(xf, norm_ffn[layer], dense_w_gate, dense_w_up, i)
            xf = matmul(u, dense_w_down, i, tm=1024, tn=256, out_dtype=F32, residual=xf, name="swiglu_down")
        else:
            normed = layer == depth - 1
            xf = moe_layer(xf, norm_ffn[layer], router[i], exp_w_gate, exp_w_up, exp_w_down, i,
                           final_norm if normed else None)
    if not normed:
        xf = rmsnorm(xf, final_norm, F32)
    return xf.reshape(b, s, d)
```

```python
import functools

import numpy as np
import jax
import jax.numpy as jnp
from jax import lax
from jax.experimental import pallas as pl
from jax.experimental.pallas import tpu as pltpu

F32 = jnp.float32
BF16 = jnp.bfloat16

GRID_W = 64
HEAD_DIM = 128
RMS_EPS = 1e-6
NA_HEADS = 6
NA_KH = 8
NA_KW = 16
NA_W = NA_HEADS * HEAD_DIM
FN_GROUPS = 6
FN_GROUP_W = 128
FN_W = FN_GROUPS * FN_GROUP_W
DIL_DILATIONS = (1, 4, 16)
DIL_HALF_WINDOW = 64
DIL_HEADS_PER_GROUP = 2
DIL_GROUP_W = DIL_HEADS_PER_GROUP * HEAD_DIM
Q_BLOCK = 128
ROT_DIM = HEAD_DIM // 4
ROPE_THETA = 500000.0
MEM_HEADS = 4
MEM_HEAD_DIM = 192
MEM_W = MEM_HEADS * MEM_HEAD_DIM
MEM_WINDOW = 256
N_EXPERTS = 8
TOP_K = 2
MOE_ROW_QUANTUM = 128
MASK_VALUE = -1e30

V7X_VMEM_BYTES = 64 * 1024 * 1024
VMEM_LIMIT_BYTES = V7X_VMEM_BYTES - 4 * 1024 * 1024
LANES = 128
SUBLANES = 8
N_DMA_PRIORITIES = 2

TILES = dict(
    norm_rows=512,
    in_proj=(2048, 768),
    kv_proj_cols=768,
    out_proj=(2048, 512),
    merge=(1024, 256),
    swiglu_up=(1024, 512),
    swiglu_down=(1024, 256),
    fourier_rows=512,
    memory_rows=512,
    route_rows=512,
    moe_rows=512,
    moe_up_cols=1024,
    moe_down_cols=512,
    moe_down_k=1024,
    combine_rows=256,
)


def _params(n_axes):
    return pltpu.CompilerParams(dimension_semantics=("arbitrary",) * n_axes,
                                vmem_limit_bytes=VMEM_LIMIT_BYTES)


def _dot(a, b):
    return jnp.dot(a, b, preferred_element_type=F32)


def _dot_nt(a, b):
    return lax.dot_general(a, b, (((1,), (1,)), ((), ())), preferred_element_type=F32)


def _softmax_pv(s, v):
    d = v.shape[1]
    top = jnp.max(s, axis=-1, keepdims=True)
    p = jnp.exp((s - top).astype(BF16))
    pv = _dot(p, jnp.concatenate([v, jnp.ones_like(v)], axis=1))
    num, den = pv[:, :d], pv[:, d:]
    return num / den, top + jnp.log(den)


def _rmsnorm_body(x, g):
    return x * lax.rsqrt(jnp.mean(x * x, axis=-1, keepdims=True) + RMS_EPS) * g


def _rmsnorm_kernel(x_ref, g_ref, o_ref):
    o_ref[...] = _rmsnorm_body(x_ref[...], g_ref[...]).astype(o_ref.dtype)


def rmsnorm(x, g, out_dtype, tm=TILES["norm_rows"]):
    m, d = x.shape
    return pl.pallas_call(
        _rmsnorm_kernel,
        grid=(m // tm,),
        in_specs=[pl.BlockSpec((tm, d), lambda i: (i, 0)),
                  pl.BlockSpec((1, d), lambda i: (0, 0))],
        out_specs=pl.BlockSpec((tm, d), lambda i: (i, 0)),
        out_shape=jax.ShapeDtypeStruct((m, d), out_dtype),
        compiler_params=_params(1),
        name="rmsnorm",
    )(x, g.reshape(1, d))


def _mm_kernel(a_ref, w_ref, o_ref):
    o_ref[...] = _dot(a_ref[...], w_ref[...].astype(BF16)).astype(o_ref.dtype)


def _mm_res_kernel(a_ref, w_ref, r_ref, o_ref):
    o_ref[...] = (r_ref[...] + _dot(a_ref[...], w_ref[...].astype(BF16))).astype(o_ref.dtype)


def matmul(a, w, layer, *, tm, tn, n_cols=None, out_dtype=BF16, residual=None, name="matmul"):
    m, k = a.shape
    n = w.shape[2] if n_cols is None else n_cols
    in_specs = [pl.BlockSpec((tm, k), lambda i, j: (i, 0)),
                pl.BlockSpec((None, k, tn), lambda i, j: (layer, 0, j))]
    args = [a, w]
    kernel = _mm_kernel
    if residual is not None:
        in_specs.append(pl.BlockSpec((tm, tn), lambda i, j: (i, j)))
        args.append(residual)
        kernel = _mm_res_kernel
    return pl.pallas_call(
        kernel,
        grid=(m // tm, n // tn),
        in_specs=in_specs,
        out_specs=pl.BlockSpec((tm, tn), lambda i, j: (i, j)),
        out_shape=jax.ShapeDtypeStruct((m, n), out_dtype),
        compiler_params=_params(2),
        name=name,
    )(*args)


NA_Q_ROWS = 4
NA_KEY_ROWS = NA_KH + NA_Q_ROWS


def _na_key_row_start(step, rows):
    return np.clip(step * NA_Q_ROWS - NA_KH // 2, 0, rows - NA_KEY_ROWS)


def _na_bias_table(rpb, rows):
    col = np.arange(GRID_W)
    col_start = np.clip(col - NA_KW // 2, 0, GRID_W - NA_KW)
    rel_col = col[None, :] - col[:, None] + (NA_KW - 1)
    col_ok = (col[None, :] >= col_start[:, None]) & (col[None, :] < col_start[:, None] + NA_KW)
    pick = (np.clip(rel_col, 0, 2 * NA_KW - 2)[None] == np.arange(2 * NA_KW - 1)[:, None, None]).astype(np.float32)
    slab = jnp.einsum("hdr,rcw->hdcw", rpb.astype(F32), pick, precision=lax.Precision.HIGHEST)
    n_steps = rows // NA_Q_ROWS
    rel_row = np.zeros((n_steps, NA_Q_ROWS, NA_KEY_ROWS), np.int64)
    row_ok = np.zeros((n_steps, NA_Q_ROWS, NA_KEY_ROWS), bool)
    for step in range(n_steps):
        for i in range(NA_Q_ROWS):
            r = step * NA_Q_ROWS + i
            win = np.clip(r - NA_KH // 2, 0, rows - NA_KH)
            key_row = _na_key_row_start(step, rows) + np.arange(NA_KEY_ROWS)
            row_ok[step, i] = (key_row >= win) & (key_row < win + NA_KH)
            rel_row[step, i] = key_row - r + (NA_KH - 1)
    assert all((rel_row[s_] * row_ok[s_] == rel_row[1] * row_ok[1]).all() and (row_ok[s_] == row_ok[1]).all()
               for s_ in range(1, n_steps - 1))
    slab = jnp.where(col_ok, slab, MASK_VALUE)
    hidden = jnp.full(slab.shape[:1] + slab.shape[2:], MASK_VALUE, F32)
    cases = []
    for step in (0, 1, n_steps - 1):
        q_rows = [jnp.concatenate([slab[:, rel_row[step, i, k]] if row_ok[step, i, k] else hidden
                                   for k in range(NA_KEY_ROWS)], axis=-1) for i in range(NA_Q_ROWS)]
        cases.append(jnp.concatenate(q_rows, axis=-2))
    return jnp.stack(cases)


def _na_kernel(q_ref, k_ref, v_ref, b_ref, o_ref, *, rows):
    step = pl.program_id(1)
    key_row0 = jnp.clip(step * NA_Q_ROWS - NA_KH // 2, 0, rows - NA_KEY_ROWS)
    k0 = pl.multiple_of(key_row0 * GRID_W, GRID_W)
    n_keys = NA_KEY_ROWS * GRID_W
    scale = HEAD_DIM ** -0.5
    for h in range(NA_HEADS):
        cols = slice(h * HEAD_DIM, (h + 1) * HEAD_DIM)
        k = k_ref[0, pl.ds(k0, n_keys), cols]
        v = v_ref[0, pl.ds(k0, n_keys), cols]
        s = _dot_nt(q_ref[0, :, cols], k) * scale + b_ref[0, h]
        o_ref[0, :, cols] = _softmax_pv(s, v)[0].astype(o_ref.dtype)


def neighbourhood_attention(z, rpb):
    b, s, _ = z.shape
    rows = s // GRID_W
    n_steps = rows // NA_Q_ROWS
    bias = _na_bias_table(rpb, rows)
    tq = NA_Q_ROWS * GRID_W
    bias_case = lambda i, r: (jnp.where(r == 0, 0, jnp.where(r == n_steps - 1, 2, 1)), 0, 0, 0)
    return pl.pallas_call(
        functools.partial(_na_kernel, rows=rows),
        grid=(b, n_steps),
        in_specs=[pl.BlockSpec((1, tq, NA_W), lambda i, r: (i, r, 0)),
                  pl.BlockSpec((1, s, NA_W), lambda i, r: (i, 0, 1)),
                  pl.BlockSpec((1, s, NA_W), lambda i, r: (i, 0, 2)),
                  pl.BlockSpec((1,) + bias.shape[1:], bias_case)],
        out_specs=pl.BlockSpec((1, tq, NA_W), lambda i, r: (i, r, 0)),
        out_shape=jax.ShapeDtypeStruct((b, s, NA_W), BF16),
        compiler_params=_params(2),
        name="neighbourhood_attention",
    )(z, z, z, bias)


def _dft_tables(s):
    half = s // 2
    n = np.arange(half, dtype=np.int64)
    ang = 2.0 * np.pi * ((n[:, None] * n[None, :]) % half) / half
    pos = np.concatenate([np.cos(ang), -np.sin(ang)], axis=1)
    c = np.arange(FN_GROUP_W, dtype=np.int64)
    ang_c = 2.0 * np.pi * ((c[:, None] * c[None, :]) % FN_GROUP_W) / FN_GROUP_W
    chan = np.concatenate([np.cos(ang_c), np.sin(ang_c)], axis=1)
    tw = np.pi * n[:, None] / half * np.ones((1, FN_GROUP_W))
    return (jnp.asarray(pos, F32).astype(BF16), jnp.asarray(chan, F32).astype(BF16),
            jnp.asarray(np.cos(tw), F32), jnp.asarray(np.sin(tw), F32))


def _fourier_kernel(u_ref, chan_ref, twc_ref, tws_ref, pos_ref, o_ref, ve_scr, vo_scr, y_scr, *, s, norm):
    half = s // 2

    @pl.when(pl.program_id(1) == 0)
    def _():
        twc, tws = twc_ref[...], tws_ref[...]
        for g in range(FN_GROUPS):
            cols = slice(g * FN_GROUP_W, (g + 1) * FN_GROUP_W)
            t = _dot(u_ref[0, :, cols], chan_ref[...])
            top, bot = t[:half], t[half:]
            fold = top + bot
            diff = top - bot
            dc, ds = diff[:, :FN_GROUP_W], diff[:, FN_GROUP_W:]
            ve_scr[0:half, cols] = fold[:, :FN_GROUP_W].astype(BF16)
            ve_scr[half:s, cols] = fold[:, FN_GROUP_W:].astype(BF16)
            vo_scr[0:half, cols] = (twc * dc - tws * ds).astype(BF16)
            vo_scr[half:s, cols] = (tws * dc + twc * ds).astype(BF16)

    rows = pos_ref.shape[0]
    y_even = _dot(pos_ref[...], ve_scr[...]) * norm
    y_odd = _dot(pos_ref[...], vo_scr[...]) * norm
    for g in range(FN_GROUPS):
        cols = slice(g * FN_GROUP_W, (g + 1) * FN_GROUP_W)
        y_scr[g, pl.ds(0, rows, stride=2), :] = y_even[:, cols]
        y_scr[g, pl.ds(1, rows, stride=2), :] = y_odd[:, cols]
        o_ref[0, :, cols] = y_scr[g].astype(o_ref.dtype)


def fourier_mix(z, ts=TILES["fourier_rows"]):
    b, s, _ = z.shape
    pos_tab, chan_tab, tw_cos, tw_sin = _dft_tables(s)
    norm = float((s * FN_GROUP_W) ** -0.5)
    whole = lambda a: pl.BlockSpec(a.shape, lambda i, j: (0, 0))
    return pl.pallas_call(
        functools.partial(_fourier_kernel, s=s, norm=norm),
        grid=(b, s // (2 * ts)),
        in_specs=[pl.BlockSpec((1, s, FN_W), lambda i, j: (i, 0, 6)),
                  whole(chan_tab), whole(tw_cos), whole(tw_sin),
                  pl.BlockSpec((ts, s), lambda i, j: (j, 0))],
        out_specs=pl.BlockSpec((1, 2 * ts, FN_W), lambda i, j: (i, j, 0)),
        out_shape=jax.ShapeDtypeStruct((b, s, FN_W), BF16),
        scratch_shapes=[pltpu.VMEM((s, FN_W), BF16), pltpu.VMEM((s, FN_W), BF16),
                        pltpu.VMEM((FN_GROUPS, 2 * ts, FN_GROUP_W), F32)],
        compiler_params=_params(2),
        name="fourier_mix",
    )(z, chan_tab, tw_cos, tw_sin, pos_tab)


def _rope_tables(s):
    half = ROT_DIM // 2
    inv_freq = ROPE_THETA ** (-2.0 * jnp.arange(half, dtype=F32) / ROT_DIM)
    ang = jnp.arange(s, dtype=jnp.int32).astype(F32)[:, None] * inv_freq[None, :]
    cos, sin = jnp.cos(ang), jnp.sin(ang)
    ones = jnp.ones((s, HEAD_DIM - ROT_DIM), F32)
    zeros_half = jnp.zeros((s, half), F32)
    zeros_rest = jnp.zeros((s, HEAD_DIM - ROT_DIM), F32)
    cos_t = jnp.concatenate([cos, cos, ones], axis=1)
    sin_lo = jnp.concatenate([-sin, zeros_half, zeros_rest], axis=1)
    sin_hi = jnp.concatenate([zeros_half, sin, zeros_rest], axis=1)
    return cos_t, sin_lo, sin_hi


def _dilated_kernel(q0, q1, q2, k0, k1, k2, v0, v1, v2, cos_ref, slo_ref, shi_ref, y_ref,
                    rq_ref, rk_ref, rv_ref, o_scr, l_scr, *, seq):
    half = ROT_DIM // 2
    scale = HEAD_DIM ** -0.5
    chunk = 4 * Q_BLOCK
    for g, (q_ref, k_ref, v_ref) in enumerate(((q0, k0, v0), (q1, k1, v1), (q2, k2, v2))):
        for c0 in range(0, seq, chunk):
            rows = slice(c0, c0 + chunk)
            cs, lo, hi = cos_ref[rows, :], slo_ref[rows, :], shi_ref[rows, :]
            for src, dst in ((q_ref, rq_ref), (k_ref, rk_ref)):
                x = src[0, rows, :].astype(F32)
                dst[g, rows, :] = x * cs + pltpu.roll(x, HEAD_DIM - half, 1) * lo + pltpu.roll(x, half, 1) * hi
            rv_ref[g, rows, :] = v_ref[0, rows, :].astype(F32)

    masks = {}

    def window_mask(n_keys, offset):
        if (n_keys, offset) not in masks:
            rel = (lax.broadcasted_iota(jnp.int32, (Q_BLOCK, n_keys), 1)
                   - lax.broadcasted_iota(jnp.int32, (Q_BLOCK, n_keys), 0) + offset)
            masks[(n_keys, offset)] = jnp.abs(rel) <= DIL_HALF_WINDOW
        return masks[(n_keys, offset)]

    for g, d in enumerate(DIL_DILATIONS):
        length = seq // d
        n_keys = min(length, Q_BLOCK + 2 * DIL_HALF_WINDOW)
        for rho in range(d):
            for m_q in range(0, length, Q_BLOCK):
                m_k = min(max(m_q - DIL_HALF_WINDOW, 0), length - n_keys)
                q_rows = pl.ds(rho + d * m_q, Q_BLOCK, stride=d) if d > 1 else pl.ds(m_q, Q_BLOCK)
                k_rows = pl.ds(rho + d * m_k, n_keys, stride=d) if d > 1 else pl.ds(m_k, n_keys)
                q = rq_ref[g, q_rows, :].astype(BF16)
                k = rk_ref[g, k_rows, :].astype(BF16)
                v = rv_ref[g, k_rows, :].astype(BF16)
                s = jnp.where(window_mask(n_keys, m_k - m_q), _dot_nt(q, k) * scale, MASK_VALUE)
                o_scr[g, q_rows, :], l_scr[g, q_rows, :] = _softmax_pv(s, v)

    n_groups = len(DIL_DILATIONS)
    for c0 in range(0, seq, chunk):
        rows = slice(c0, c0 + chunk)
        lses = [l_scr[g, rows, :] for g in range(n_groups)]
        top = functools.reduce(jnp.maximum, lses)
        ws = [jnp.exp(l - top) for l in lses]
        num = sum(w * o_scr[g, rows, :] for g, w in enumerate(ws))
        y_ref[0, rows, :] = (num / sum(ws)).astype(y_ref.dtype)


def dilated_attention(z):
    b, s, _ = z.shape
    n_groups = len(DIL_DILATIONS)
    tables = _rope_tables(s)
    heads_per_block = NA_W // HEAD_DIM

    def head_spec(t, g):
        first = (3 + t) * heads_per_block + g * DIL_HEADS_PER_GROUP
        return pl.BlockSpec((1, s, HEAD_DIM), lambda i, h: (i, 0, first + h))

    table_spec = pl.BlockSpec((s, HEAD_DIM), lambda i, h: (0, 0))
    scratch = pltpu.VMEM((n_groups, s, HEAD_DIM), F32)
    return pl.pallas_call(
        functools.partial(_dilated_kernel, seq=s),
        grid=(b, DIL_HEADS_PER_GROUP),
        in_specs=[head_spec(t, g) for t in range(3) for g in range(n_groups)] + [table_spec] * 3,
        out_specs=pl.BlockSpec((1, s, HEAD_DIM), lambda i, h: (i, 0, h)),
        out_shape=jax.ShapeDtypeStruct((b, s, DIL_GROUP_W), BF16),
        scratch_shapes=[scratch] * 5,
        compiler_params=_params(2),
        name="dilated_attention",
    )(*([z] * 9), *tables).reshape(b * s, DIL_GROUP_W)


def _mem_attn_kernel(q_ref, k_ref, v_ref, o_ref):
    scale = MEM_HEAD_DIM ** -0.5
    col = lax.broadcasted_iota(jnp.int32, (1, MEM_WINDOW), 1)
    parts = []
    for h in range(MEM_HEADS):
        start = h * MEM_HEAD_DIM // LANES * LANES
        win = slice(start, start + MEM_WINDOW)
        in_head = (col >= h * MEM_HEAD_DIM - start) & (col < (h + 1) * MEM_HEAD_DIM - start)
        q = q_ref[0, :, win]
        s = _dot_nt(jnp.where(in_head, q, jnp.zeros_like(q)), k_ref[0, :, win]) * scale
        o = jnp.where(in_head, _softmax_pv(s, v_ref[0, :, win])[0], 0.0)
        pads = (start, MEM_W - start - MEM_WINDOW)
        left, right = (jnp.zeros((o.shape[0], n), F32) for n in pads)
        parts.append(jnp.concatenate([a for a in (left, o, right) if a.shape[1]], axis=1))
    o_ref[0] = sum(parts).astype(o_ref.dtype)


def memory_attention(z, kv, tq=TILES["memory_rows"]):
    b, s, _ = z.shape
    m = kv.shape[1]
    return pl.pallas_call(
        _mem_attn_kernel,
        grid=(b, s // tq),
        in_specs=[pl.BlockSpec((1, tq, MEM_W), lambda i, j: (i, j, 7)),
                  pl.BlockSpec((1, m, MEM_W), lambda i, j: (i, 0, 0)),
                  pl.BlockSpec((1, m, MEM_W), lambda i, j: (i, 0, 1))],
        out_specs=pl.BlockSpec((1, tq, MEM_W), lambda i, j: (i, j, 0)),
        out_shape=jax.ShapeDtypeStruct((b, s, MEM_W), BF16),
        compiler_params=_params(2),
        name="memory_attention",
    )(z, kv, kv)


def _merge_kernel(h_ref, ya_ref, yb_ref, yc_ref, ym_ref, wg0, wg1, wg2, wg3, bg0, bg1, bg2, bg3,
                  wa_ref, wb_ref, wc_ref, wm_ref, o_ref):
    h = h_ref[...]
    acc = None
    for y_ref, wg, bg, wbr in ((ya_ref, wg0, bg0, wa_ref), (yb_ref, wg1, bg1, wb_ref),
                               (yc_ref, wg2, bg2, wc_ref), (ym_ref, wg3, bg3, wm_ref)):
        gate = jax.nn.sigmoid(_dot(h, wg[...].astype(BF16)) + bg[...])
        term = gate * _dot(y_ref[...], wbr[...].astype(BF16))
        acc = term if acc is None else acc + term
    o_ref[...] = acc.astype(o_ref.dtype)


def _gate_block(i, j, *, layer, branch, n_tiles):
    return (layer, 0, branch * n_tiles + j)


def gated_merge(h, ys, w_gate, b_gate, w_brs, layers, tm=TILES["merge"][0], tn=TILES["merge"][1]):
    m, d = h.shape
    n_tiles = d // tn
    row = lambda width: pl.BlockSpec((tm, width), lambda i, j: (i, 0))
    gate_maps = [functools.partial(_gate_block, layer=layers[0], branch=br, n_tiles=n_tiles) for br in range(4)]
    gate_w = [pl.BlockSpec((None, d, tn), gm) for gm in gate_maps]
    gate_b = [pl.BlockSpec((None, 1, tn), gm) for gm in gate_maps]
    br_w = [pl.BlockSpec((None, w.shape[1], tn), functools.partial(lambda i, j, l: (l, 0, j), l=l))
            for w, l in zip(w_brs, layers[1:])]
    b_gate3 = b_gate.reshape(b_gate.shape[0], 1, -1)
    return pl.pallas_call(
        _merge_kernel,
        grid=(m // tm, n_tiles),
        in_specs=[row(d)] + [row(y.shape[1]) for y in ys] + gate_w + gate_b + br_w,
        out_specs=pl.BlockSpec((tm, tn), lambda i, j: (i, j)),
        out_shape=jax.ShapeDtypeStruct((m, d), BF16),
        compiler_params=_params(2),
        name="gated_merge",
    )(h, *ys, w_gate, w_gate, w_gate, w_gate, b_gate3, b_gate3, b_gate3, b_gate3, *w_brs)


def _silu_mul(g, u):
    return g * jax.nn.sigmoid(g) * u


def _swiglu_up_kernel(x_ref, g_ref, wg_ref, wu_ref, o_ref, h_scr):
    @pl.when(pl.program_id(1) == 0)
    def _():
        h_scr[...] = _rmsnorm_body(x_ref[...], g_ref[...]).astype(h_scr.dtype)

    a = h_scr[...]
    o_ref[...] = _silu_mul(_dot(a, wg_ref[...].astype(BF16)), _dot(a, wu_ref[...].astype(BF16))).astype(o_ref.dtype)


def swiglu_up(x, norm_g, wg, wu, layer, tm=TILES["swiglu_up"][0], tf=TILES["swiglu_up"][1]):
    m, k = x.shape
    f = wg.shape[2]
    return pl.pallas_call(
        _swiglu_up_kernel,
        grid=(m // tm, f // tf),
        in_specs=[pl.BlockSpec((tm, k), lambda i, j: (i, 0)),
                  pl.BlockSpec((1, k), lambda i, j: (0, 0)),
                  pl.BlockSpec((None, k, tf), lambda i, j: (layer, 0, j)),
                  pl.BlockSpec((None, k, tf), lambda i, j: (layer, 0, j))],
        out_specs=pl.BlockSpec((tm, tf), lambda i, j: (i, j)),
        out_shape=jax.ShapeDtypeStruct((m, f), BF16),
        scratch_shapes=[pltpu.VMEM((tm, k), BF16)],
        compiler_params=_params(2),
        name="swiglu_up",
    )(x, norm_g.reshape(1, k), wg, wu)


def _router_kernel(x_ref, g_ref, r_ref, idx_ref, rank_ref, gate_ref, cnt_ref):
    @pl.when(pl.program_id(0) == 0)
    def _():
        cnt_ref[...] = jnp.zeros_like(cnt_ref)

    h = _rmsnorm_body(x_ref[...], g_ref[...])
    r = r_ref[...]
    h_hi, r_hi = h.astype(BF16), r.astype(BF16)
    h_lo, r_lo = (h - h_hi.astype(F32)).astype(BF16), (r - r_hi.astype(F32)).astype(BF16)
    logits = _dot(h_hi, r_hi) + (_dot(h_hi, r_lo) + _dot(h_lo, r_hi))
    lane = lax.broadcasted_iota(jnp.int32, logits.shape, 1).astype(F32)
    neg = -jnp.inf
    lg = jnp.where(lane < N_EXPERTS, logits, neg)
    v1 = jnp.max(lg, axis=-1, keepdims=True)
    i1 = jnp.min(jnp.where(lg == v1, lane, float(LANES)), axis=-1, keepdims=True)
    lg2 = jnp.where(lane == i1, neg, lg)
    v2 = jnp.max(lg2, axis=-1, keepdims=True)
    i2 = jnp.min(jnp.where(lg2 == v2, lane, float(LANES)), axis=-1, keepdims=True)
    e = jnp.exp(v2 - v1)
    g1 = 1.0 / (1.0 + e)
    g2 = e / (1.0 + e)
    onehot = jnp.where((lane == i1) | (lane == i2), 1.0, 0.0)
    tt = onehot.shape[0]
    earlier = lax.broadcasted_iota(jnp.int32, (tt, tt), 1) < lax.broadcasted_iota(jnp.int32, (tt, tt), 0)
    pos = _dot(jnp.where(earlier, 1.0, 0.0).astype(BF16), onehot.astype(BF16)) + cnt_ref[...]
    r1 = jnp.sum(jnp.where(lane == i1, pos, 0.0), axis=-1, keepdims=True)
    r2 = jnp.sum(jnp.where(lane == i2, pos, 0.0), axis=-1, keepdims=True)
    cnt_ref[...] += jnp.sum(onehot, axis=0, keepdims=True)
    idx_ref[...] = jnp.where(lane == 0, i1, jnp.where(lane == 1, i2, 0.0)).astype(jnp.int32)
    rank_ref[...] = jnp.where(lane == 0, r1, jnp.where(lane == 1, r2, 0.0)).astype(jnp.int32)
    gate_ref[...] = jnp.where(lane == 0, g1, jnp.where(lane == 1, g2, 0.0))


def route(x, g, router, tm=TILES["route_rows"]):
    m, d = x.shape
    r_pad = jnp.pad(router, ((0, 0), (0, LANES - N_EXPERTS)))
    lane_spec = pl.BlockSpec((tm, LANES), lambda i: (i, 0))
    return pl.pallas_call(
        _router_kernel,
        grid=(m // tm,),
        in_specs=[pl.BlockSpec((tm, d), lambda i: (i, 0)),
                  pl.BlockSpec((1, d), lambda i: (0, 0)),
                  pl.BlockSpec((d, LANES), lambda i: (0, 0))],
        out_specs=[lane_spec, lane_spec, lane_spec, pl.BlockSpec((1, LANES), lambda i: (0, 0))],
        out_shape=[jax.ShapeDtypeStruct((m, LANES), jnp.int32),
                   jax.ShapeDtypeStruct((m, LANES), jnp.int32),
                   jax.ShapeDtypeStruct((m, LANES), F32),
                   jax.ShapeDtypeStruct((1, LANES), F32)],
        compiler_params=_params(1),
        name="route",
    )(x, g.reshape(1, d), r_pad)


def _dispatch_kernel(dest_ref, lo_ref, hi_ref, na_ref, x_hbm, g_ref, a_ref, src_ref, rows_ref, sems):
    j = pl.program_id(0)
    n_groups8, d = rows_ref.shape[1], rows_ref.shape[3]
    tm = n_groups8 * SUBLANES
    n_active = na_ref[0]

    def row_copy(tile, i, u):
        slot = lax.rem(tile, 2)
        return pltpu.make_async_copy(x_hbm.at[pl.ds(src_ref[tile * tm + SUBLANES * i + u], 1)],
                                     rows_ref.at[slot, i, pl.ds(u, 1)], sems.at[slot])

    def start_tile(tile):
        def start(i, c):
            for u in range(SUBLANES):
                row_copy(tile, i, u).start(priority=u % N_DMA_PRIORITIES)
            return c

        lax.fori_loop(0, n_groups8, start, 0)

    @pl.when(j == 0)
    def _():
        def clear(i, c):
            src_ref[i] = 0
            return c

        for e in range(N_EXPERTS):
            lax.fori_loop(lo_ref[e], hi_ref[e], clear, 0)

        def fill(i, c):
            src_ref[dest_ref[i]] = lax.shift_right_logical(i, 1)
            return c

        lax.fori_loop(0, dest_ref.shape[0], fill, 0, unroll=8)

        @pl.when(n_active > 0)
        def _():
            start_tile(0)

    @pl.when(j + 1 < n_active)
    def _():
        start_tile(j + 1)

    @pl.when(j < n_active)
    def _():
        def wait(i, c):
            for u in range(SUBLANES):
                row_copy(j, i, u).wait()
            return c

        lax.fori_loop(0, n_groups8, wait, 0)
        rows = rows_ref[lax.rem(j, 2)].reshape(tm, d)
        a_ref[...] = _rmsnorm_body(rows, g_ref[...]).astype(a_ref.dtype)

    @pl.when(j >= n_active)
    def _():
        a_ref[...] = jnp.zeros_like(a_ref)


def _combine_kernel(dest_ref, x_ref, gate_ref, g_ref, y_hbm, o_ref, rows_ref, sems, *, apply_norm):
    tt, d = x_ref.shape
    i = pl.program_id(0)

    def row_copy(tile, r8, u, k):
        slot = lax.rem(tile, 2)
        row = dest_ref[TOP_K * (tile * tt + SUBLANES * r8 + u) + k]
        return pltpu.make_async_copy(y_hbm.at[pl.ds(row, 1)], rows_ref.at[slot, k, r8, pl.ds(u, 1)], sems.at[slot])

    def start_tile(tile):
        def start(r8, c):
            for u in range(SUBLANES):
                for k in range(TOP_K):
                    row_copy(tile, r8, u, k).start(priority=k % N_DMA_PRIORITIES)
            return c

        lax.fori_loop(0, tt // SUBLANES, start, 0)

    @pl.when(i == 0)
    def _():
        start_tile(0)

    @pl.when(i + 1 < pl.num_programs(0))
    def _():
        start_tile(i + 1)

    def wait(r8, c):
        for u in range(SUBLANES):
            for k in range(TOP_K):
                row_copy(i, r8, u, k).wait()
        return c

    lax.fori_loop(0, tt // SUBLANES, wait, 0)
    gates = gate_ref[...]
    slot = lax.rem(i, 2)
    y = x_ref[...]
    for k in range(TOP_K):
        y = y + gates[:, k:k + 1] * rows_ref[slot, k].reshape(tt, d)
    o_ref[...] = _rmsnorm_body(y, g_ref[...]) if apply_norm else y


def _on_valid_rows(n_valid, o_ref, compute):
    tm = o_ref.shape[0]
    for n in range(MOE_ROW_QUANTUM, tm + 1, MOE_ROW_QUANTUM):
        @pl.when((n_valid > n - MOE_ROW_QUANTUM) & (n_valid <= n))
        def _(n=n):
            o_ref[:n, :] = compute(n).astype(o_ref.dtype)
            if n < tm:
                o_ref[n:, :] = jnp.zeros((tm - n, o_ref.shape[1]), o_ref.dtype)

    @pl.when(n_valid == 0)
    def _():
        o_ref[...] = jnp.zeros_like(o_ref)


def _stream_expert_weights(meta, w_hbms, w_scr, sems, layer, width):
    te_ref, first_ref, group_ref, next_ref, n_groups_ref = meta
    c, j = pl.program_id(0), pl.program_id(1)
    group = group_ref[j]
    n_groups = n_groups_ref[0]
    block = c * n_groups + group
    slot = lax.rem(block, 2)

    def copies(expert, col, to_slot):
        cols = pl.ds(pl.multiple_of(col * width, width), width)
        return [pltpu.make_async_copy(w.at[layer, expert, :, cols], w_scr.at[to_slot, i], sems.at[to_slot, i])
                for i, w in enumerate(w_hbms)]

    @pl.when(first_ref[j] == 1)
    def _():
        @pl.when(block == 0)
        def _():
            for cp in copies(te_ref[j], c, slot):
                cp.start()

        for cp in copies(te_ref[j], c, slot):
            cp.wait()
        more_groups = group + 1 < n_groups

        @pl.when(more_groups)
        def _():
            for cp in copies(next_ref[j], c, 1 - slot):
                cp.start()

        @pl.when(jnp.logical_not(more_groups) & (c + 1 < pl.num_programs(0)))
        def _():
            for cp in copies(te_ref[0], c + 1, 1 - slot):
                cp.start()

    return slot


def _moe_up_kernel(te_ref, na_ref, nv_ref, first_ref, group_ref, next_ref, ng_ref, a_ref, wg_hbm, wu_hbm, o_ref,
                   w_scr, sems, *, layer):
    meta = (te_ref, first_ref, group_ref, next_ref, ng_ref)
    slot = _stream_expert_weights(meta, (wg_hbm, wu_hbm), w_scr, sems, layer, o_ref.shape[1])

    def compute(n):
        a = a_ref[:n, :]
        return _silu_mul(_dot(a, w_scr[slot, 0].astype(BF16)), _dot(a, w_scr[slot, 1].astype(BF16)))

    _on_valid_rows(nv_ref[pl.program_id(1)], o_ref, compute)


def _moe_down_kernel(te_ref, na_ref, nv_ref, first_ref, group_ref, next_ref, ng_ref, a_ref, w_hbm, o_ref,
                     w_scr, sems, *, layer, tk):
    meta = (te_ref, first_ref, group_ref, next_ref, ng_ref)
    slot = _stream_expert_weights(meta, (w_hbm,), w_scr, sems, layer, o_ref.shape[1])

    def compute(n):
        acc = None
        for c in range(a_ref.shape[1] // tk):
            part = _dot(a_ref[:n, c * tk:(c + 1) * tk], w_scr[slot, 0, c * tk:(c + 1) * tk, :].astype(BF16))
            acc = part if acc is None else acc + part
        return acc

    _on_valid_rows(nv_ref[pl.program_id(1)], o_ref, compute)


def moe_layer(x, norm_g, router, wg, wu, wd, layer, out_g, tm=TILES["moe_rows"], tf=TILES["moe_up_cols"],
              tn=TILES["moe_down_cols"], tk=TILES["moe_down_k"], tt=TILES["combine_rows"]):
    assert TOP_K == 2
    t, d = x.shape
    _, n_exp, _, f = wg.shape
    n_rows = t * TOP_K
    n_tiles = n_rows // tm + n_exp
    idx, rank, gate, cnt = route(x, norm_g, router)
    counts = cnt[0, :n_exp].astype(jnp.int32)
    tiles_per = (counts + tm - 1) // tm
    tile_end = jnp.cumsum(tiles_per)
    start = (tile_end - tiles_per) * tm
    n_active = tile_end[-1:]
    tile_ids = jnp.arange(n_tiles, dtype=jnp.int32)
    tile_expert = jnp.sum((tile_ids[:, None] >= tile_end[None, :]).astype(jnp.int32), axis=1)
    last_expert = jnp.sum((n_active - 1 >= tile_end).astype(jnp.int32))
    tile_expert = jnp.where(tile_ids < n_active, tile_expert, last_expert).astype(jnp.int32)
    group_end = start + counts
    tile_valid = jnp.where(tile_ids < n_active, jnp.clip(group_end[tile_expert] - tile_ids * tm, 0, tm), 0)
    prev_expert = jnp.concatenate([jnp.full((1,), -1, jnp.int32), tile_expert[:-1]])
    tile_first = ((tile_ids < n_active) & (tile_expert != prev_expert)).astype(jnp.int32)
    tile_group = jnp.cumsum(tile_first) - 1
    n_groups = jnp.sum(tile_first, keepdims=True)
    group_expert = jnp.argsort(tiles_per == 0, stable=True).astype(jnp.int32)
    tile_next = group_expert[jnp.minimum(tile_group + 1, n_exp - 1)]
    tile_meta = (tile_expert, n_active, tile_valid, tile_first, tile_group, tile_next, n_groups)
    picked = idx[:, :TOP_K, None] == jnp.arange(n_exp, dtype=jnp.int32)
    dest = (jnp.sum(jnp.where(picked, start, 0), axis=-1) + rank[:, :TOP_K]).reshape(n_rows)

    row_tile = lambda j, na: jnp.maximum(jnp.minimum(j, na[0] - 1), 0)
    a = pl.pallas_call(
        _dispatch_kernel,
        grid_spec=pltpu.PrefetchScalarGridSpec(
            num_scalar_prefetch=4,
            grid=(n_tiles,),
            in_specs=[pl.BlockSpec(memory_space=pl.ANY),
                      pl.BlockSpec((1, d), lambda j, dst, lo, hi, na: (0, 0))],
            out_specs=pl.BlockSpec((tm, d), lambda j, dst, lo, hi, na: (j, 0)),
            scratch_shapes=[pltpu.SMEM((n_tiles * tm,), jnp.int32),
                            pltpu.VMEM((2, tm // SUBLANES, SUBLANES, d), F32),
                            pltpu.SemaphoreType.DMA((2,))]),
        out_shape=jax.ShapeDtypeStruct((n_tiles * tm, d), BF16),
        compiler_params=_params(1),
        name="moe_dispatch",
    )(dest, group_end, tile_end * tm, n_active, x, norm_g.reshape(1, d))
    lhs_tile = lambda c, j, te, na, *_: (row_tile(j, na), 0)
    out_tile = lambda c, j, *_: (j, c)
    in_hbm = pl.BlockSpec(memory_space=pl.ANY)
    up = pl.pallas_call(
        functools.partial(_moe_up_kernel, layer=layer),
        grid_spec=pltpu.PrefetchScalarGridSpec(
            num_scalar_prefetch=len(tile_meta),
            grid=(f // tf, n_tiles),
            in_specs=[pl.BlockSpec((tm, d), lhs_tile), in_hbm, in_hbm],
            out_specs=pl.BlockSpec((tm, tf), out_tile),
            scratch_shapes=[pltpu.VMEM((2, 2, d, tf), F32), pltpu.SemaphoreType.DMA((2, 2))]),
        out_shape=jax.ShapeDtypeStruct((n_tiles * tm, f), BF16),
        compiler_params=_params(2),
        name="moe_up",
    )(*tile_meta, a, wg, wu)
    down = pl.pallas_call(
        functools.partial(_moe_down_kernel, layer=layer, tk=tk),
        grid_spec=pltpu.PrefetchScalarGridSpec(
            num_scalar_prefetch=len(tile_meta),
            grid=(d // tn, n_tiles),
            in_specs=[pl.BlockSpec((tm, f), lhs_tile), in_hbm],
            out_specs=pl.BlockSpec((tm, tn), out_tile),
            scratch_shapes=[pltpu.VMEM((2, 1, f, tn), F32), pltpu.SemaphoreType.DMA((2, 1))]),
        out_shape=jax.ShapeDtypeStruct((n_tiles * tm, d), F32),
        compiler_params=_params(2),
        name="moe_down",
    )(*tile_meta, up, wd)
    g_out = jnp.ones((d,), F32) if out_g is None else out_g
    return pl.pallas_call(
        functools.partial(_combine_kernel, apply_norm=out_g is not None),
        grid_spec=pltpu.PrefetchScalarGridSpec(
            num_scalar_prefetch=1,
            grid=(t // tt,),
            in_specs=[pl.BlockSpec((tt, d), lambda i, dst: (i, 0)),
                      pl.BlockSpec((tt, LANES), lambda i, dst: (i, 0)),
                      pl.BlockSpec((1, d), lambda i, dst: (0, 0)),
                      pl.BlockSpec(memory_space=pl.ANY)],
            out_specs=pl.BlockSpec((tt, d), lambda i, dst: (i, 0)),
            scratch_shapes=[pltpu.VMEM((2, TOP_K, tt // SUBLANES, SUBLANES, d), F32),
                            pltpu.SemaphoreType.DMA((2,))]),
        out_shape=jax.ShapeDtypeStruct((t, d), F32),
        compiler_params=_params(1),
        name="moe_combine",
    )(dest, x, gate, g_out.reshape(1, d), down)


def _mixer(x, mem_n, b, s, layer, w_in, rpb, w_mem_kv, w_br_a, w_br_b, w_br_c, w_br_m, w_gate, b_gate, w_out,
           norm_g):
    t = b * s
    h = rmsnorm(x, norm_g, BF16)
    assert TILES["in_proj"][1] == NA_W == FN_W == MEM_W
    z = matmul(h, w_in, layer, tm=TILES["in_proj"][0], tn=TILES["in_proj"][1], name="in_proj").reshape(b, s, -1)
    kv_m = matmul(mem_n, w_mem_kv, layer, tm=mem_n.shape[0], tn=TILES["kv_proj_cols"], name="mem_kv_proj")
    y_a = neighbourhood_attention(z, rpb).reshape(t, NA_W)
    y_b = fourier_mix(z).reshape(t, FN_W)
    y_c = dilated_attention(z)
    y_m = memory_attention(z, kv_m.reshape(b, -1, 2 * MEM_W)).reshape(t, MEM_W)
    merged = gated_merge(h, (y_a, y_b, y_c, y_m), w_gate, b_gate, (w_br_a, w_br_b, w_br_c, w_br_m),
                         (layer,) * 5)
    return matmul(merged, w_out, layer, tm=TILES["out_proj"][0], tn=TILES["out_proj"][1], out_dtype=F32,
                  residual=x, name="out_proj")


def kernel(x, mem, norm_mix, w_in, rpb, norm_mem, w_mem_kv, w_br_a, w_br_b, w_br_c, w_br_m, w_gate, b_gate,
           w_out, norm_ffn, dense_w_gate, dense_w_up, dense_w_down, router, exp_w_gate, exp_w_up, exp_w_down,
           final_norm):
    b, s, d = x.shape
    depth = norm_mix.shape[0]
    t = b * s
    xf = x.reshape(t, d)
    memf = mem.reshape(-1, d)
    normed = False
    for layer in range(depth):
        mem_n = rmsnorm(memf, norm_mem[layer], BF16)
        xf = _mixer(xf, mem_n, b, s, layer, w_in, rpb[layer], w_mem_kv, w_br_a, w_br_b, w_br_c, w_br_m, w_gate,
                    b_gate, w_out, norm_mix[layer])
        i = layer // 2
        if layer % 2 == 0:
            u = swiglu_up(xf, norm_ffn[layer], dense_w_gate, dense_w_up, i)
            xf = matmul(u, dense_w_down, i, tm=TILES["swiglu_down"][0], tn=TILES["swiglu_down"][1],
                        out_dtype=F32, residual=xf, name="swiglu_down")
        else:
            normed = layer == depth - 1
            xf = moe_layer(xf, norm_ffn[layer], router[i], exp_w_gate, exp_w_up, exp_w_down, i,
                           final_norm if normed else None)
    if not normed:
        xf = rmsnorm(xf, final_norm, F32)
    return xf.reshape(b, s, d)
```

```python
import functools

import numpy as np
import jax
import jax.numpy as jnp
from jax import lax
from jax.experimental import pallas as pl
from jax.experimental.pallas import tpu as pltpu

F32 = jnp.float32
BF16 = jnp.bfloat16

GRID_W = 64
HEAD_DIM = 128
RMS_EPS = 1e-6
NA_HEADS = 6
NA_KH = 8
NA_KW = 16
NA_W = NA_HEADS * HEAD_DIM
FN_GROUPS = 6
FN_GROUP_W = 128
FN_W = FN_GROUPS * FN_GROUP_W
DIL_DILATIONS = (1, 4, 16)
DIL_HALF_WINDOW = 64
DIL_HEADS_PER_GROUP = 2
DIL_GROUP_W = DIL_HEADS_PER_GROUP * HEAD_DIM
Q_BLOCK = 128
ROT_DIM = HEAD_DIM // 4
ROPE_THETA = 500000.0
MEM_HEADS = 4
MEM_HEAD_DIM = 192
MEM_W = MEM_HEADS * MEM_HEAD_DIM
MEM_WINDOW = 256
N_EXPERTS = 8
TOP_K = 2
MOE_ROW_QUANTUM = 128
MASK_VALUE = -1e30

V7X_VMEM_BYTES = 64 * 1024 * 1024
VMEM_LIMIT_BYTES = V7X_VMEM_BYTES - 4 * 1024 * 1024
LANES = 128
SUBLANES = 8
N_DMA_PRIORITIES = 2

TILES = dict(
    norm_rows=512,
    in_proj=(1024, 768),
    kv_proj_cols=768,
    out_proj=(2048, 512),
    merge=(1024, 256),
    swiglu_up=(1024, 512),
    swiglu_down=(1024, 256),
    fourier_rows=512,
    memory_rows=512,
    route_rows=512,
    moe_rows=512,
    moe_up_cols=1024,
    moe_down_cols=512,
    moe_down_k=1024,
    combine_rows=256,
)


def _params(n_axes):
    return pltpu.CompilerParams(dimension_semantics=("arbitrary",) * n_axes,
                                vmem_limit_bytes=VMEM_LIMIT_BYTES)


def _dot(a, b):
    return jnp.dot(a, b, preferred_element_type=F32)


def _dot_nt(a, b):
    return lax.dot_general(a, b, (((1,), (1,)), ((), ())), preferred_element_type=F32)


def _softmax_pv(s, v):
    d = v.shape[1]
    top = jnp.max(s, axis=-1, keepdims=True)
    p = jnp.exp((s - top).astype(BF16))
    pv = _dot(p, jnp.concatenate([v, jnp.ones_like(v)], axis=1))
    num, den = pv[:, :d], pv[:, d:]
    return num / den, top + jnp.log(den)


def _rmsnorm_body(x, g):
    return x * lax.rsqrt(jnp.mean(x * x, axis=-1, keepdims=True) + RMS_EPS) * g


def _rmsnorm_kernel(x_ref, g_ref, o_ref):
    o_ref[...] = _rmsnorm_body(x_ref[...], g_ref[...]).astype(o_ref.dtype)


def rmsnorm(x, g, out_dtype, tm=TILES["norm_rows"]):
    m, d = x.shape
    return pl.pallas_call(
        _rmsnorm_kernel,
        grid=(m // tm,),
        in_specs=[pl.BlockSpec((tm, d), lambda i: (i, 0)),
                  pl.BlockSpec((1, d), lambda i: (0, 0))],
        out_specs=pl.BlockSpec((tm, d), lambda i: (i, 0)),
        out_shape=jax.ShapeDtypeStruct((m, d), out_dtype),
        compiler_params=_params(1),
        name="rmsnorm",
    )(x, g.reshape(1, d))


def _mm_kernel(a_ref, w_ref, o_ref):
    o_ref[...] = _dot(a_ref[...], w_ref[...].astype(BF16)).astype(o_ref.dtype)


def _mm_res_kernel(a_ref, w_ref, r_ref, o_ref):
    o_ref[...] = (r_ref[...] + _dot(a_ref[...], w_ref[...].astype(BF16))).astype(o_ref.dtype)


def matmul(a, w, layer, *, tm, tn, n_cols=None, out_dtype=BF16, residual=None, name="matmul"):
    m, k = a.shape
    n = w.shape[2] if n_cols is None else n_cols
    in_specs = [pl.BlockSpec((tm, k), lambda i, j: (i, 0)),
                pl.BlockSpec((None, k, tn), lambda i, j: (layer, 0, j))]
    args = [a, w]
    kernel = _mm_kernel
    if residual is not None:
        in_specs.append(pl.BlockSpec((tm, tn), lambda i, j: (i, j)))
        args.append(residual)
        kernel = _mm_res_kernel
    return pl.pallas_call(
        kernel,
        grid=(m // tm, n // tn),
        in_specs=in_specs,
        out_specs=pl.BlockSpec((tm, tn), lambda i, j: (i, j)),
        out_shape=jax.ShapeDtypeStruct((m, n), out_dtype),
        compiler_params=_params(2),
        name=name,
    )(*args)


def _norm_mm_kernel(x_ref, g_ref, w_ref, o_ref, h_ref, h_scr):
    @pl.when(pl.program_id(1) == 0)
    def _():
        h_scr[...] = _rmsnorm_body(x_ref[...], g_ref[...]).astype(h_scr.dtype)
        h_ref[...] = h_scr[...]

    o_ref[...] = _dot(h_scr[...], w_ref[...].astype(BF16)).astype(o_ref.dtype)


def norm_matmul(x, norm_g, w, layer, *, tm, tn, name):
    m, k = x.shape
    n = w.shape[2]
    return pl.pallas_call(
        _norm_mm_kernel,
        grid=(m // tm, n // tn),
        in_specs=[pl.BlockSpec((tm, k), lambda i, j: (i, 0)),
                  pl.BlockSpec((1, k), lambda i, j: (0, 0)),
                  pl.BlockSpec((None, k, tn), lambda i, j: (layer, 0, j))],
        out_specs=[pl.BlockSpec((tm, tn), lambda i, j: (i, j)),
                   pl.BlockSpec((tm, k), lambda i, j: (i, 0))],
        out_shape=[jax.ShapeDtypeStruct((m, n), BF16), jax.ShapeDtypeStruct((m, k), BF16)],
        scratch_shapes=[pltpu.VMEM((tm, k), BF16)],
        compiler_params=_params(2),
        name=name,
    )(x, norm_g.reshape(1, k), w)


NA_Q_ROWS = 4
NA_KEY_ROWS = NA_KH + NA_Q_ROWS


def _na_key_row_start(step, rows):
    return np.clip(step * NA_Q_ROWS - NA_KH // 2, 0, rows - NA_KEY_ROWS)


def _na_bias_table(rpb, rows):
    col = np.arange(GRID_W)
    col_start = np.clip(col - NA_KW // 2, 0, GRID_W - NA_KW)
    rel_col = col[None, :] - col[:, None] + (NA_KW - 1)
    col_ok = (col[None, :] >= col_start[:, None]) & (col[None, :] < col_start[:, None] + NA_KW)
    pick = (np.clip(rel_col, 0, 2 * NA_KW - 2)[None] == np.arange(2 * NA_KW - 1)[:, None, None]).astype(np.float32)
    slab = jnp.einsum("hdr,rcw->hdcw", rpb.astype(F32), pick, precision=lax.Precision.HIGHEST)
    n_steps = rows // NA_Q_ROWS
    rel_row = np.zeros((n_steps, NA_Q_ROWS, NA_KEY_ROWS), np.int64)
    row_ok = np.zeros((n_steps, NA_Q_ROWS, NA_KEY_ROWS), bool)
    for step in range(n_steps):
        for i in range(NA_Q_ROWS):
            r = step * NA_Q_ROWS + i
            win = np.clip(r - NA_KH // 2, 0, rows - NA_KH)
            key_row = _na_key_row_start(step, rows) + np.arange(NA_KEY_ROWS)
            row_ok[step, i] = (key_row >= win) & (key_row < win + NA_KH)
            rel_row[step, i] = key_row - r + (NA_KH - 1)
    assert all((rel_row[s_] * row_ok[s_] == rel_row[1] * row_ok[1]).all() and (row_ok[s_] == row_ok[1]).all()
               for s_ in range(1, n_steps - 1))
    slab = jnp.where(col_ok, slab, MASK_VALUE)
    hidden = jnp.full(slab.shape[:1] + slab.shape[2:], MASK_VALUE, F32)
    cases = []
    for step in (0, 1, n_steps - 1):
        q_rows = [jnp.concatenate([slab[:, rel_row[step, i, k]] if row_ok[step, i, k] else hidden
                                   for k in range(NA_KEY_ROWS)], axis=-1) for i in range(NA_Q_ROWS)]
        cases.append(jnp.concatenate(q_rows, axis=-2))
    return jnp.stack(cases)


def _na_kernel(q_ref, k_ref, v_ref, b_ref, o_ref, *, rows):
    step = pl.program_id(1)
    key_row0 = jnp.clip(step * NA_Q_ROWS - NA_KH // 2, 0, rows - NA_KEY_ROWS)
    k0 = pl.multiple_of(key_row0 * GRID_W, GRID_W)
    n_keys = NA_KEY_ROWS * GRID_W
    scale = HEAD_DIM ** -0.5
    for h in range(NA_HEADS):
        cols = slice(h * HEAD_DIM, (h + 1) * HEAD_DIM)
        k = k_ref[0, pl.ds(k0, n_keys), cols]
        v = v_ref[0, pl.ds(k0, n_keys), cols]
        s = _dot_nt(q_ref[0, :, cols], k) * scale + b_ref[0, h]
        o_ref[0, :, cols] = _softmax_pv(s, v)[0].astype(o_ref.dtype)


def neighbourhood_attention(z, rpb):
    b, s, _ = z.shape
    rows = s // GRID_W
    n_steps = rows // NA_Q_ROWS
    bias = _na_bias_table(rpb, rows)
    tq = NA_Q_ROWS * GRID_W
    bias_case = lambda i, r: (jnp.where(r == 0, 0, jnp.where(r == n_steps - 1, 2, 1)), 0, 0, 0)
    return pl.pallas_call(
        functools.partial(_na_kernel, rows=rows),
        grid=(b, n_steps),
        in_specs=[pl.BlockSpec((1, tq, NA_W), lambda i, r: (i, r, 0)),
                  pl.BlockSpec((1, s, NA_W), lambda i, r: (i, 0, 1)),
                  pl.BlockSpec((1, s, NA_W), lambda i, r: (i, 0, 2)),
                  pl.BlockSpec((1,) + bias.shape[1:], bias_case)],
        out_specs=pl.BlockSpec((1, tq, NA_W), lambda i, r: (i, r, 0)),
        out_shape=jax.ShapeDtypeStruct((b, s, NA_W), BF16),
        compiler_params=_params(2),
        name="neighbourhood_attention",
    )(z, z, z, bias)


def _dft_tables(s):
    half = s // 2
    n = np.arange(half, dtype=np.int64)
    ang = 2.0 * np.pi * ((n[:, None] * n[None, :]) % half) / half
    pos = np.concatenate([np.cos(ang), -np.sin(ang)], axis=1)
    c = np.arange(FN_GROUP_W, dtype=np.int64)
    ang_c = 2.0 * np.pi * ((c[:, None] * c[None, :]) % FN_GROUP_W) / FN_GROUP_W
    chan = np.concatenate([np.cos(ang_c), np.sin(ang_c)], axis=1)
    tw = np.pi * n[:, None] / half * np.ones((1, FN_GROUP_W))
    return (jnp.asarray(pos, F32).astype(BF16), jnp.asarray(chan, F32).astype(BF16),
            jnp.asarray(np.cos(tw), F32), jnp.asarray(np.sin(tw), F32))


def _fourier_kernel(u_ref, chan_ref, twc_ref, tws_ref, pos_ref, o_ref, ve_scr, vo_scr, y_scr, *, s, norm):
    half = s // 2

    @pl.when(pl.program_id(1) == 0)
    def _():
        twc, tws = twc_ref[...], tws_ref[...]
        for g in range(FN_GROUPS):
            cols = slice(g * FN_GROUP_W, (g + 1) * FN_GROUP_W)
            t = _dot(u_ref[0, :, cols], chan_ref[...])
            top, bot = t[:half], t[half:]
            fold = top + bot
            diff = top - bot
            dc, ds = diff[:, :FN_GROUP_W], diff[:, FN_GROUP_W:]
            ve_scr[0:half, cols] = fold[:, :FN_GROUP_W].astype(BF16)
            ve_scr[half:s, cols] = fold[:, FN_GROUP_W:].astype(BF16)
            vo_scr[0:half, cols] = (twc * dc - tws * ds).astype(BF16)
            vo_scr[half:s, cols] = (tws * dc + twc * ds).astype(BF16)

    rows = pos_ref.shape[0]
    y_even = _dot(pos_ref[...], ve_scr[...]) * norm
    y_odd = _dot(pos_ref[...], vo_scr[...]) * norm
    for g in range(FN_GROUPS):
        cols = slice(g * FN_GROUP_W, (g + 1) * FN_GROUP_W)
        y_scr[g, pl.ds(0, rows, stride=2), :] = y_even[:, cols]
        y_scr[g, pl.ds(1, rows, stride=2), :] = y_odd[:, cols]
        o_ref[0, :, cols] = y_scr[g].astype(o_ref.dtype)


def fourier_mix(z, ts=TILES["fourier_rows"]):
    b, s, _ = z.shape
    pos_tab, chan_tab, tw_cos, tw_sin = _dft_tables(s)
    norm = float((s * FN_GROUP_W) ** -0.5)
    whole = lambda a: pl.BlockSpec(a.shape, lambda i, j: (0, 0))
    return pl.pallas_call(
        functools.partial(_fourier_kernel, s=s, norm=norm),
        grid=(b, s // (2 * ts)),
        in_specs=[pl.BlockSpec((1, s, FN_W), lambda i, j: (i, 0, 6)),
                  whole(chan_tab), whole(tw_cos), whole(tw_sin),
                  pl.BlockSpec((ts, s), lambda i, j: (j, 0))],
        out_specs=pl.BlockSpec((1, 2 * ts, FN_W), lambda i, j: (i, j, 0)),
        out_shape=jax.ShapeDtypeStruct((b, s, FN_W), BF16),
        scratch_shapes=[pltpu.VMEM((s, FN_W), BF16), pltpu.VMEM((s, FN_W), BF16),
                        pltpu.VMEM((FN_GROUPS, 2 * ts, FN_GROUP_W), F32)],
        compiler_params=_params(2),
        name="fourier_mix",
    )(z, chan_tab, tw_cos, tw_sin, pos_tab)


def _rope_tables(s):
    half = ROT_DIM // 2
    inv_freq = ROPE_THETA ** (-2.0 * jnp.arange(half, dtype=F32) / ROT_DIM)
    ang = jnp.arange(s, dtype=jnp.int32).astype(F32)[:, None] * inv_freq[None, :]
    cos, sin = jnp.cos(ang), jnp.sin(ang)
    ones = jnp.ones((s, HEAD_DIM - ROT_DIM), F32)
    zeros_half = jnp.zeros((s, half), F32)
    zeros_rest = jnp.zeros((s, HEAD_DIM - ROT_DIM), F32)
    cos_t = jnp.concatenate([cos, cos, ones], axis=1)
    sin_lo = jnp.concatenate([-sin, zeros_half, zeros_rest], axis=1)
    sin_hi = jnp.concatenate([zeros_half, sin, zeros_rest], axis=1)
    return cos_t, sin_lo, sin_hi


def _dilated_kernel(q0, q1, q2, k0, k1, k2, v0, v1, v2, cos_ref, slo_ref, shi_ref, y_ref,
                    rq_ref, rk_ref, rv_ref, o_scr, l_scr, *, seq):
    half = ROT_DIM // 2
    scale = HEAD_DIM ** -0.5
    chunk = 4 * Q_BLOCK
    for g, (q_ref, k_ref, v_ref) in enumerate(((q0, k0, v0), (q1, k1, v1), (q2, k2, v2))):
        for c0 in range(0, seq, chunk):
            rows = slice(c0, c0 + chunk)
            cs, lo, hi = cos_ref[rows, :], slo_ref[rows, :], shi_ref[rows, :]
            for src, dst in ((q_ref, rq_ref), (k_ref, rk_ref)):
                x = src[0, rows, :].astype(F32)
                dst[g, rows, :] = x * cs + pltpu.roll(x, HEAD_DIM - half, 1) * lo + pltpu.roll(x, half, 1) * hi
            rv_ref[g, rows, :] = v_ref[0, rows, :].astype(F32)

    masks = {}

    def window_mask(n_keys, offset):
        if (n_keys, offset) not in masks:
            rel = (lax.broadcasted_iota(jnp.int32, (Q_BLOCK, n_keys), 1)
                   - lax.broadcasted_iota(jnp.int32, (Q_BLOCK, n_keys), 0) + offset)
            masks[(n_keys, offset)] = jnp.abs(rel) <= DIL_HALF_WINDOW
        return masks[(n_keys, offset)]

    for g, d in enumerate(DIL_DILATIONS):
        length = seq // d
        n_keys = min(length, Q_BLOCK + 2 * DIL_HALF_WINDOW)
        for rho in range(d):
            for m_q in range(0, length, Q_BLOCK):
                m_k = min(max(m_q - DIL_HALF_WINDOW, 0), length - n_keys)
                q_rows = pl.ds(rho + d * m_q, Q_BLOCK, stride=d) if d > 1 else pl.ds(m_q, Q_BLOCK)
                k_rows = pl.ds(rho + d * m_k, n_keys, stride=d) if d > 1 else pl.ds(m_k, n_keys)
                q = rq_ref[g, q_rows, :].astype(BF16)
                k = rk_ref[g, k_rows, :].astype(BF16)
                v = rv_ref[g, k_rows, :].astype(BF16)
                s = jnp.where(window_mask(n_keys, m_k - m_q), _dot_nt(q, k) * scale, MASK_VALUE)
                o_scr[g, q_rows, :], l_scr[g, q_rows, :] = _softmax_pv(s, v)

    n_groups = len(DIL_DILATIONS)
    for c0 in range(0, seq, chunk):
        rows = slice(c0, c0 + chunk)
        lses = [l_scr[g, rows, :] for g in range(n_groups)]
        top = functools.reduce(jnp.maximum, lses)
        ws = [jnp.exp(l - top) for l in lses]
        num = sum(w * o_scr[g, rows, :] for g, w in enumerate(ws))
        y_ref[0, rows, :] = (num / sum(ws)).astype(y_ref.dtype)


def dilated_attention(z):
    b, s, _ = z.shape
    n_groups = len(DIL_DILATIONS)
    tables = _rope_tables(s)
    heads_per_block = NA_W // HEAD_DIM

    def head_spec(t, g):
        first = (3 + t) * heads_per_block + g * DIL_HEADS_PER_GROUP
        return pl.BlockSpec((1, s, HEAD_DIM), lambda i, h: (i, 0, first + h))

    table_spec = pl.BlockSpec((s, HEAD_DIM), lambda i, h: (0, 0))
    scratch = pltpu.VMEM((n_groups, s, HEAD_DIM), F32)
    return pl.pallas_call(
        functools.partial(_dilated_kernel, seq=s),
        grid=(b, DIL_HEADS_PER_GROUP),
        in_specs=[head_spec(t, g) for t in range(3) for g in range(n_groups)] + [table_spec] * 3,
        out_specs=pl.BlockSpec((1, s, HEAD_DIM), lambda i, h: (i, 0, h)),
        out_shape=jax.ShapeDtypeStruct((b, s, DIL_GROUP_W), BF16),
        scratch_shapes=[scratch] * 5,
        compiler_params=_params(2),
        name="dilated_attention",
    )(*([z] * 9), *tables).reshape(b * s, DIL_GROUP_W)


def _mem_attn_kernel(q_ref, k_ref, v_ref, o_ref):
    scale = MEM_HEAD_DIM ** -0.5
    col = lax.broadcasted_iota(jnp.int32, (1, MEM_WINDOW), 1)
    parts = []
    for h in range(MEM_HEADS):
        start = h * MEM_HEAD_DIM // LANES * LANES
        win = slice(start, start + MEM_WINDOW)
        in_head = (col >= h * MEM_HEAD_DIM - start) & (col < (h + 1) * MEM_HEAD_DIM - start)
        q = q_ref[0, :, win]
        s = _dot_nt(jnp.where(in_head, q, jnp.zeros_like(q)), k_ref[0, :, win]) * scale
        o = jnp.where(in_head, _softmax_pv(s, v_ref[0, :, win])[0], 0.0)
        pads = (start, MEM_W - start - MEM_WINDOW)
        left, right = (jnp.zeros((o.shape[0], n), F32) for n in pads)
        parts.append(jnp.concatenate([a for a in (left, o, right) if a.shape[1]], axis=1))
    o_ref[0] = sum(parts).astype(o_ref.dtype)


def memory_attention(z, kv, tq=TILES["memory_rows"]):
    b, s, _ = z.shape
    m = kv.shape[1]
    return pl.pallas_call(
        _mem_attn_kernel,
        grid=(b, s // tq),
        in_specs=[pl.BlockSpec((1, tq, MEM_W), lambda i, j: (i, j, 7)),
                  pl.BlockSpec((1, m, MEM_W), lambda i, j: (i, 0, 0)),
                  pl.BlockSpec((1, m, MEM_W), lambda i, j: (i, 0, 1))],
        out_specs=pl.BlockSpec((1, tq, MEM_W), lambda i, j: (i, j, 0)),
        out_shape=jax.ShapeDtypeStruct((b, s, MEM_W), BF16),
        compiler_params=_params(2),
        name="memory_attention",
    )(z, kv, kv)


def _merge_kernel(h_ref, ya_ref, yb_ref, yc_ref, ym_ref, wg0, wg1, wg2, wg3, bg0, bg1, bg2, bg3,
                  wa_ref, wb_ref, wc_ref, wm_ref, o_ref):
    h = h_ref[...]
    acc = None
    for y_ref, wg, bg, wbr in ((ya_ref, wg0, bg0, wa_ref), (yb_ref, wg1, bg1, wb_ref),
                               (yc_ref, wg2, bg2, wc_ref), (ym_ref, wg3, bg3, wm_ref)):
        gate = jax.nn.sigmoid(_dot(h, wg[...].astype(BF16)) + bg[...])
        term = gate * _dot(y_ref[...], wbr[...].astype(BF16))
        acc = term if acc is None else acc + term
    o_ref[...] = acc.astype(o_ref.dtype)


def _gate_block(i, j, *, layer, branch, n_tiles):
    return (layer, 0, branch * n_tiles + j)


def gated_merge(h, ys, w_gate, b_gate, w_brs, layers, tm=TILES["merge"][0], tn=TILES["merge"][1]):
    m, d = h.shape
    n_tiles = d // tn
    row = lambda width: pl.BlockSpec((tm, width), lambda i, j: (i, 0))
    gate_maps = [functools.partial(_gate_block, layer=layers[0], branch=br, n_tiles=n_tiles) for br in range(4)]
    gate_w = [pl.BlockSpec((None, d, tn), gm) for gm in gate_maps]
    gate_b = [pl.BlockSpec((None, 1, tn), gm) for gm in gate_maps]
    br_w = [pl.BlockSpec((None, w.shape[1], tn), functools.partial(lambda i, j, l: (l, 0, j), l=l))
            for w, l in zip(w_brs, layers[1:])]
    b_gate3 = b_gate.reshape(b_gate.shape[0], 1, -1)
    return pl.pallas_call(
        _merge_kernel,
        grid=(m // tm, n_tiles),
        in_specs=[row(d)] + [row(y.shape[1]) for y in ys] + gate_w + gate_b + br_w,
        out_specs=pl.BlockSpec((tm, tn), lambda i, j: (i, j)),
        out_shape=jax.ShapeDtypeStruct((m, d), BF16),
        compiler_params=_params(2),
        name="gated_merge",
    )(h, *ys, w_gate, w_gate, w_gate, w_gate, b_gate3, b_gate3, b_gate3, b_gate3, *w_brs)


def _silu_mul(g, u):
    return g * jax.nn.sigmoid(g) * u


def _swiglu_up_kernel(x_ref, g_ref, wg_ref, wu_ref, o_ref, h_scr):
    @pl.when(pl.program_id(1) == 0)
    def _():
        h_scr[...] = _rmsnorm_body(x_ref[...], g_ref[...]).astype(h_scr.dtype)

    a = h_scr[...]
    o_ref[...] = _silu_mul(_dot(a, wg_ref[...].astype(BF16)), _dot(a, wu_ref[...].astype(BF16))).astype(o_ref.dtype)


def swiglu_up(x, norm_g, wg, wu, layer, tm=TILES["swiglu_up"][0], tf=TILES["swiglu_up"][1]):
    m, k = x.shape
    f = wg.shape[2]
    return pl.pallas_call(
        _swiglu_up_kernel,
        grid=(m // tm, f // tf),
        in_specs=[pl.BlockSpec((tm, k), lambda i, j: (i, 0)),
                  pl.BlockSpec((1, k), lambda i, j: (0, 0)),
                  pl.BlockSpec((None, k, tf), lambda i, j: (layer, 0, j)),
                  pl.BlockSpec((None, k, tf), lambda i, j: (layer, 0, j))],
        out_specs=pl.BlockSpec((tm, tf), lambda i, j: (i, j)),
        out_shape=jax.ShapeDtypeStruct((m, f), BF16),
        scratch_shapes=[pltpu.VMEM((tm, k), BF16)],
        compiler_params=_params(2),
        name="swiglu_up",
    )(x, norm_g.reshape(1, k), wg, wu)


def _router_kernel(x_ref, g_ref, r_ref, idx_ref, rank_ref, gate_ref, cnt_ref):
    @pl.when(pl.program_id(0) == 0)
    def _():
        cnt_ref[...] = jnp.zeros_like(cnt_ref)

    h = _rmsnorm_body(x_ref[...], g_ref[...])
    r = r_ref[...]
    h_hi, r_hi = h.astype(BF16), r.astype(BF16)
    h_lo, r_lo = (h - h_hi.astype(F32)).astype(BF16), (r - r_hi.astype(F32)).astype(BF16)
    logits = _dot(h_hi, r_hi) + (_dot(h_hi, r_lo) + _dot(h_lo, r_hi))
    lane = lax.broadcasted_iota(jnp.int32, logits.shape, 1).astype(F32)
    neg = -jnp.inf
    lg = jnp.where(lane < N_EXPERTS, logits, neg)
    v1 = jnp.max(lg, axis=-1, keepdims=True)
    i1 = jnp.min(jnp.where(lg == v1, lane, float(LANES)), axis=-1, keepdims=True)
    lg2 = jnp.where(lane == i1, neg, lg)
    v2 = jnp.max(lg2, axis=-1, keepdims=True)
    i2 = jnp.min(jnp.where(lg2 == v2, lane, float(LANES)), axis=-1, keepdims=True)
    e = jnp.exp(v2 - v1)
    g1 = 1.0 / (1.0 + e)
    g2 = e / (1.0 + e)
    onehot = jnp.where((lane == i1) | (lane == i2), 1.0, 0.0)
    tt = onehot.shape[0]
    earlier = lax.broadcasted_iota(jnp.int32, (tt, tt), 1) < lax.broadcasted_iota(jnp.int32, (tt, tt), 0)
    pos = _dot(jnp.where(earlier, 1.0, 0.0).astype(BF16), onehot.astype(BF16)) + cnt_ref[...]
    r1 = jnp.sum(jnp.where(lane == i1, pos, 0.0), axis=-1, keepdims=True)
    r2 = jnp.sum(jnp.where(lane == i2, pos, 0.0), axis=-1, keepdims=True)
    cnt_ref[...] += jnp.sum(onehot, axis=0, keepdims=True)
    idx_ref[...] = jnp.where(lane == 0, i1, jnp.where(lane == 1, i2, 0.0)).astype(jnp.int32)
    rank_ref[...] = jnp.where(lane == 0, r1, jnp.where(lane == 1, r2, 0.0)).astype(jnp.int32)
    gate_ref[...] = jnp.where(lane == 0, g1, jnp.where(lane == 1, g2, 0.0))


def route(x, g, router, tm=TILES["route_rows"]):
    m, d = x.shape
    r_pad = jnp.pad(router, ((0, 0), (0, LANES - N_EXPERTS)))
    lane_spec = pl.BlockSpec((tm, LANES), lambda i: (i, 0))
    return pl.pallas_call(
        _router_kernel,
        grid=(m // tm,),
        in_specs=[pl.BlockSpec((tm, d), lambda i: (i, 0)),
                  pl.BlockSpec((1, d), lambda i: (0, 0)),
                  pl.BlockSpec((d, LANES), lambda i: (0, 0))],
        out_specs=[lane_spec, lane_spec, lane_spec, pl.BlockSpec((1, LANES), lambda i: (0, 0))],
        out_shape=[jax.ShapeDtypeStruct((m, LANES), jnp.int32),
                   jax.ShapeDtypeStruct((m, LANES), jnp.int32),
                   jax.ShapeDtypeStruct((m, LANES), F32),
                   jax.ShapeDtypeStruct((1, LANES), F32)],
        compiler_params=_params(1),
        name="route",
    )(x, g.reshape(1, d), r_pad)


def _dispatch_kernel(dest_ref, lo_ref, hi_ref, na_ref, x_hbm, g_ref, a_ref, src_ref, rows_ref, sems):
    j = pl.program_id(0)
    n_groups8, d = rows_ref.shape[1], rows_ref.shape[3]
    tm = n_groups8 * SUBLANES
    n_active = na_ref[0]

    def row_copy(tile, i, u):
        slot = lax.rem(tile, 2)
        return pltpu.make_async_copy(x_hbm.at[pl.ds(src_ref[tile * tm + SUBLANES * i + u], 1)],
                                     rows_ref.at[slot, i, pl.ds(u, 1)], sems.at[slot])

    def start_tile(tile):
        def start(i, c):
            for u in range(SUBLANES):
                row_copy(tile, i, u).start(priority=u % N_DMA_PRIORITIES)
            return c

        lax.fori_loop(0, n_groups8, start, 0)

    @pl.when(j == 0)
    def _():
        def clear(i, c):
            src_ref[i] = 0
            return c

        for e in range(N_EXPERTS):
            lax.fori_loop(lo_ref[e], hi_ref[e], clear, 0)

        def fill(i, c):
            src_ref[dest_ref[i]] = lax.shift_right_logical(i, 1)
            return c

        lax.fori_loop(0, dest_ref.shape[0], fill, 0, unroll=8)

        @pl.when(n_active > 0)
        def _():
            start_tile(0)

    @pl.when(j + 1 < n_active)
    def _():
        start_tile(j + 1)

    @pl.when(j < n_active)
    def _():
        def wait(i, c):
            for u in range(SUBLANES):
                row_copy(j, i, u).wait()
            return c

        lax.fori_loop(0, n_groups8, wait, 0)
        rows = rows_ref[lax.rem(j, 2)].reshape(tm, d)
        a_ref[...] = _rmsnorm_body(rows, g_ref[...]).astype(a_ref.dtype)

    @pl.when(j >= n_active)
    def _():
        a_ref[...] = jnp.zeros_like(a_ref)


def _combine_kernel(dest_ref, x_ref, gate_ref, g_ref, y_hbm, o_ref, rows_ref, sems, *, apply_norm):
    tt, d = x_ref.shape
    i = pl.program_id(0)

    def row_copy(tile, r8, u, k):
        slot = lax.rem(tile, 2)
        row = dest_ref[TOP_K * (tile * tt + SUBLANES * r8 + u) + k]
        return pltpu.make_async_copy(y_hbm.at[pl.ds(row, 1)], rows_ref.at[slot, k, r8, pl.ds(u, 1)], sems.at[slot])

    def start_tile(tile):
        def start(r8, c):
            for u in range(SUBLANES):
                for k in range(TOP_K):
                    row_copy(tile, r8, u, k).start(priority=k % N_DMA_PRIORITIES)
            return c

        lax.fori_loop(0, tt // SUBLANES, start, 0)

    @pl.when(i == 0)
    def _():
        start_tile(0)

    @pl.when(i + 1 < pl.num_programs(0))
    def _():
        start_tile(i + 1)

    def wait(r8, c):
        for u in range(SUBLANES):
            for k in range(TOP_K):
                row_copy(i, r8, u, k).wait()
        return c

    lax.fori_loop(0, tt // SUBLANES, wait, 0)
    gates = gate_ref[...]
    slot = lax.rem(i, 2)
    y = x_ref[...]
    for k in range(TOP_K):
        y = y + gates[:, k:k + 1] * rows_ref[slot, k].reshape(tt, d)
    o_ref[...] = _rmsnorm_body(y, g_ref[...]) if apply_norm else y


def _on_valid_rows(n_valid, o_ref, compute):
    tm = o_ref.shape[0]
    for n in range(MOE_ROW_QUANTUM, tm + 1, MOE_ROW_QUANTUM):
        @pl.when((n_valid > n - MOE_ROW_QUANTUM) & (n_valid <= n))
        def _(n=n):
            o_ref[:n, :] = compute(n).astype(o_ref.dtype)
            if n < tm:
                o_ref[n:, :] = jnp.zeros((tm - n, o_ref.shape[1]), o_ref.dtype)

    @pl.when(n_valid == 0)
    def _():
        o_ref[...] = jnp.zeros_like(o_ref)


def _stream_expert_weights(meta, w_hbms, w_scr, sems, layer, width):
    te_ref, first_ref, group_ref, next_ref, n_groups_ref = meta
    c, j = pl.program_id(0), pl.program_id(1)
    group = group_ref[j]
    n_groups = n_groups_ref[0]
    block = c * n_groups + group
    slot = lax.rem(block, 2)

    def copies(expert, col, to_slot):
        cols = pl.ds(pl.multiple_of(col * width, width), width)
        return [pltpu.make_async_copy(w.at[layer, expert, :, cols], w_scr.at[to_slot, i], sems.at[to_slot, i])
                for i, w in enumerate(w_hbms)]

    @pl.when(first_ref[j] == 1)
    def _():
        @pl.when(block == 0)
        def _():
            for cp in copies(te_ref[j], c, slot):
                cp.start()

        for cp in copies(te_ref[j], c, slot):
            cp.wait()
        more_groups = group + 1 < n_groups

        @pl.when(more_groups)
        def _():
            for cp in copies(next_ref[j], c, 1 - slot):
                cp.start()

        @pl.when(jnp.logical_not(more_groups) & (c + 1 < pl.num_programs(0)))
        def _():
            for cp in copies(te_ref[0], c + 1, 1 - slot):
                cp.start()

    return slot


def _moe_up_kernel(te_ref, na_ref, nv_ref, first_ref, group_ref, next_ref, ng_ref, a_ref, wg_hbm, wu_hbm, o_ref,
                   w_scr, sems, *, layer):
    meta = (te_ref, first_ref, group_ref, next_ref, ng_ref)
    slot = _stream_expert_weights(meta, (wg_hbm, wu_hbm), w_scr, sems, layer, o_ref.shape[1])

    def compute(n):
        a = a_ref[:n, :]
        return _silu_mul(_dot(a, w_scr[slot, 0].astype(BF16)), _dot(a, w_scr[slot, 1].astype(BF16)))

    _on_valid_rows(nv_ref[pl.program_id(1)], o_ref, compute)


def _moe_down_kernel(te_ref, na_ref, nv_ref, first_ref, group_ref, next_ref, ng_ref, a_ref, w_hbm, o_ref,
                     w_scr, sems, *, layer, tk):
    meta = (te_ref, first_ref, group_ref, next_ref, ng_ref)
    slot = _stream_expert_weights(meta, (w_hbm,), w_scr, sems, layer, o_ref.shape[1])

    def compute(n):
        acc = None
        for c in range(a_ref.shape[1] // tk):
            part = _dot(a_ref[:n, c * tk:(c + 1) * tk], w_scr[slot, 0, c * tk:(c + 1) * tk, :].astype(BF16))
            acc = part if acc is None else acc + part
        return acc

    _on_valid_rows(nv_ref[pl.program_id(1)], o_ref, compute)


def moe_layer(x, norm_g, router, wg, wu, wd, layer, out_g, tm=TILES["moe_rows"], tf=TILES["moe_up_cols"],
              tn=TILES["moe_down_cols"], tk=TILES["moe_down_k"], tt=TILES["combine_rows"]):
    assert TOP_K == 2
    t, d = x.shape
    _, n_exp, _, f = wg.shape
    n_rows = t * TOP_K
    n_tiles = n_rows // tm + n_exp
    idx, rank, gate, cnt = route(x, norm_g, router)
    counts = cnt[0, :n_exp].astype(jnp.int32)
    tiles_per = (counts + tm - 1) // tm
    tile_end = jnp.cumsum(tiles_per)
    start = (tile_end - tiles_per) * tm
    n_active = tile_end[-1:]
    tile_ids = jnp.arange(n_tiles, dtype=jnp.int32)
    tile_expert = jnp.sum((tile_ids[:, None] >= tile_end[None, :]).astype(jnp.int32), axis=1)
    last_expert = jnp.sum((n_active - 1 >= tile_end).astype(jnp.int32))
    tile_expert = jnp.where(tile_ids < n_active, tile_expert, last_expert).astype(jnp.int32)
    group_end = start + counts
    tile_valid = jnp.where(tile_ids < n_active, jnp.clip(group_end[tile_expert] - tile_ids * tm, 0, tm), 0)
    prev_expert = jnp.concatenate([jnp.full((1,), -1, jnp.int32), tile_expert[:-1]])
    tile_first = ((tile_ids < n_active) & (tile_expert != prev_expert)).astype(jnp.int32)
    tile_group = jnp.cumsum(tile_first) - 1
    n_groups = jnp.sum(tile_first, keepdims=True)
    group_expert = jnp.argsort(tiles_per == 0, stable=True).astype(jnp.int32)
    tile_next = group_expert[jnp.minimum(tile_group + 1, n_exp - 1)]
    tile_meta = (tile_expert, n_active, tile_valid, tile_first, tile_group, tile_next, n_groups)
    picked = idx[:, :TOP_K, None] == jnp.arange(n_exp, dtype=jnp.int32)
    dest = (jnp.sum(jnp.where(picked, start, 0), axis=-1) + rank[:, :TOP_K]).reshape(n_rows)

    row_tile = lambda j, na: jnp.maximum(jnp.minimum(j, na[0] - 1), 0)
    a = pl.pallas_call(
        _dispatch_kernel,
        grid_spec=pltpu.PrefetchScalarGridSpec(
            num_scalar_prefetch=4,
            grid=(n_tiles,),
            in_specs=[pl.BlockSpec(memory_space=pl.ANY),
                      pl.BlockSpec((1, d), lambda j, dst, lo, hi, na: (0, 0))],
            out_specs=pl.BlockSpec((tm, d), lambda j, dst, lo, hi, na: (j, 0)),
            scratch_shapes=[pltpu.SMEM((n_tiles * tm,), jnp.int32),
                            pltpu.VMEM((2, tm // SUBLANES, SUBLANES, d), F32),
                            pltpu.SemaphoreType.DMA((2,))]),
        out_shape=jax.ShapeDtypeStruct((n_tiles * tm, d), BF16),
        compiler_params=_params(1),
        name="moe_dispatch",
    )(dest, group_end, tile_end * tm, n_active, x, norm_g.reshape(1, d))
    lhs_tile = lambda c, j, te, na, *_: (row_tile(j, na), 0)
    out_tile = lambda c, j, *_: (j, c)
    in_hbm = pl.BlockSpec(memory_space=pl.ANY)
    up = pl.pallas_call(
        functools.partial(_moe_up_kernel, layer=layer),
        grid_spec=pltpu.PrefetchScalarGridSpec(
            num_scalar_prefetch=len(tile_meta),
            grid=(f // tf, n_tiles),
            in_specs=[pl.BlockSpec((tm, d), lhs_tile), in_hbm, in_hbm],
            out_specs=pl.BlockSpec((tm, tf), out_tile),
            scratch_shapes=[pltpu.VMEM((2, 2, d, tf), F32), pltpu.SemaphoreType.DMA((2, 2))]),
        out_shape=jax.ShapeDtypeStruct((n_tiles * tm, f), BF16),
        compiler_params=_params(2),
        name="moe_up",
    )(*tile_meta, a, wg, wu)
    down = pl.pallas_call(
        functools.partial(_moe_down_kernel, layer=layer, tk=tk),
        grid_spec=pltpu.PrefetchScalarGridSpec(
            num_scalar_prefetch=len(tile_meta),
            grid=(d // tn, n_tiles),
            in_specs=[pl.BlockSpec((tm, f), lhs_tile), in_hbm],
            out_specs=pl.BlockSpec((tm, tn), out_tile),
            scratch_shapes=[pltpu.VMEM((2, 1, f, tn), F32), pltpu.SemaphoreType.DMA((2, 1))]),
        out_shape=jax.ShapeDtypeStruct((n_tiles * tm, d), F32),
        compiler_params=_params(2),
        name="moe_down",
    )(*tile_meta, up, wd)
    g_out = jnp.ones((d,), F32) if out_g is None else out_g
    return pl.pallas_call(
        functools.partial(_combine_kernel, apply_norm=out_g is not None),
        grid_spec=pltpu.PrefetchScalarGridSpec(
            num_scalar_prefetch=1,
            grid=(t // tt,),
            in_specs=[pl.BlockSpec((tt, d), lambda i, dst: (i, 0)),
                      pl.BlockSpec((tt, LANES), lambda i, dst: (i, 0)),
                      pl.BlockSpec((1, d), lambda i, dst: (0, 0)),
                      pl.BlockSpec(memory_space=pl.ANY)],
            out_specs=pl.BlockSpec((tt, d), lambda i, dst: (i, 0)),
            scratch_shapes=[pltpu.VMEM((2, TOP_K, tt // SUBLANES, SUBLANES, d), F32),
                            pltpu.SemaphoreType.DMA((2,))]),
        out_shape=jax.ShapeDtypeStruct((t, d), F32),
        compiler_params=_params(1),
        name="moe_combine",
    )(dest, x, gate, g_out.reshape(1, d), down)


def _mixer(x, mem_n, b, s, layer, w_in, rpb, w_mem_kv, w_br_a, w_br_b, w_br_c, w_br_m, w_gate, b_gate, w_out,
           norm_g):
    t = b * s
    assert TILES["in_proj"][1] == NA_W == FN_W == MEM_W
    z, h = norm_matmul(x, norm_g, w_in, layer, tm=TILES["in_proj"][0], tn=TILES["in_proj"][1], name="in_proj")
    z = z.reshape(b, s, -1)
    kv_m = matmul(mem_n, w_mem_kv, layer, tm=mem_n.shape[0], tn=TILES["kv_proj_cols"], name="mem_kv_proj")
    y_a = neighbourhood_attention(z, rpb).reshape(t, NA_W)
    y_b = fourier_mix(z).reshape(t, FN_W)
    y_c = dilated_attention(z)
    y_m = memory_attention(z, kv_m.reshape(b, -1, 2 * MEM_W)).reshape(t, MEM_W)
    merged = gated_merge(h, (y_a, y_b, y_c, y_m), w_gate, b_gate, (w_br_a, w_br_b, w_br_c, w_br_m),
                         (layer,) * 5)
    return matmul(merged, w_out, layer, tm=TILES["out_proj"][0], tn=TILES["out_proj"][1], out_dtype=F32,
                  residual=x, name="out_proj")


def kernel(x, mem, norm_mix, w_in, rpb, norm_mem, w_mem_kv, w_br_a, w_br_b, w_br_c, w_br_m, w_gate, b_gate,
           w_out, norm_ffn, dense_w_gate, dense_w_up, dense_w_down, router, exp_w_gate, exp_w_up, exp_w_down,
           final_norm):
    b, s, d = x.shape
    depth = norm_mix.shape[0]
    t = b * s
    xf = x.reshape(t, d)
    memf = mem.reshape(-1, d)
    normed = False
    for layer in range(depth):
        mem_n = rmsnorm(memf, norm_mem[layer], BF16)
        xf = _mixer(xf, mem_n, b, s, layer, w_in, rpb[layer], w_mem_kv, w_br_a, w_br_b, w_br_c, w_br_m, w_gate,
                    b_gate, w_out, norm_mix[layer])
        i = layer // 2
        if layer % 2 == 0:
            u = swiglu_up(xf, norm_ffn[layer], dense_w_gate, dense_w_up, i)
            xf = matmul(u, dense_w_down, i, tm=TILES["swiglu_down"][0], tn=TILES["swiglu_down"][1],
                        out_dtype=F32, residual=xf, name="swiglu_down")
        else:
            normed = layer == depth - 1
            xf = moe_layer(xf, norm_ffn[layer], router[i], exp_w_gate, exp_w_up, exp_w_down, i,
                           final_norm if normed else None)
    if not normed:
        xf = rmsnorm(xf, final_norm, F32)
    return xf.reshape(b, s, d)
```

```python
import functools

import numpy as np
import jax
import jax.numpy as jnp
from jax import lax
from jax.experimental import pallas as pl
from jax.experimental.pallas import tpu as pltpu

F32 = jnp.float32
BF16 = jnp.bfloat16

GRID_W = 64
HEAD_DIM = 128
RMS_EPS = 1e-6
NA_HEADS = 6
NA_KH = 8
NA_KW = 16
NA_W = NA_HEADS * HEAD_DIM
FN_GROUPS = 6
FN_GROUP_W = 128
FN_W = FN_GROUPS * FN_GROUP_W
DIL_DILATIONS = (1, 4, 16)
DIL_HALF_WINDOW = 64
DIL_HEADS_PER_GROUP = 2
DIL_GROUP_W = DIL_HEADS_PER_GROUP * HEAD_DIM
Q_BLOCK = 128
ROT_DIM = HEAD_DIM // 4
ROPE_THETA = 500000.0
MEM_HEADS = 4
MEM_HEAD_DIM = 192
MEM_W = MEM_HEADS * MEM_HEAD_DIM
MEM_WINDOW = 256
N_EXPERTS = 8
TOP_K = 2
MOE_DOWN_K_SPLITS = 2
MOE_ROW_QUANTUM = 128
MASK_VALUE = -1e30

V7X_VMEM_BYTES = 64 * 1024 * 1024
VMEM_LIMIT_BYTES = V7X_VMEM_BYTES - 4 * 1024 * 1024
LANES = 128
SUBLANES = 8
N_DMA_PRIORITIES = 2

TILES = dict(
    norm_rows=512,
    in_proj=(2048, 768),
    kv_proj_cols=768,
    out_proj=(2048, 512),
    merge=(1024, 256),
    swiglu_up=(1024, 512),
    swiglu_down=(1024, 256),
    fourier_rows=512,
    memory_rows=512,
    route_rows=512,
    moe_rows=512,
    moe_up_cols=1024,
    moe_down_cols=1024,
    moe_down_k=896,
    combine_rows=256,
)


def _params(n_axes):
    return pltpu.CompilerParams(dimension_semantics=("arbitrary",) * n_axes,
                                vmem_limit_bytes=VMEM_LIMIT_BYTES)


def _dot(a, b):
    return jnp.dot(a, b, preferred_element_type=F32)


def _dot_nt(a, b):
    return lax.dot_general(a, b, (((1,), (1,)), ((), ())), preferred_element_type=F32)


def _softmax_pv(s, v):
    d = v.shape[1]
    top = jnp.max(s, axis=-1, keepdims=True)
    p = jnp.exp((s - top).astype(BF16))
    pv = _dot(p, jnp.concatenate([v, jnp.ones_like(v)], axis=1))
    num, den = pv[:, :d], pv[:, d:]
    return num / den, top + jnp.log(den)


def _rmsnorm_body(x, g):
    return x * lax.rsqrt(jnp.mean(x * x, axis=-1, keepdims=True) + RMS_EPS) * g


def _rmsnorm_kernel(x_ref, g_ref, o_ref):
    o_ref[...] = _rmsnorm_body(x_ref[...], g_ref[...]).astype(o_ref.dtype)


def rmsnorm(x, g, out_dtype, tm=TILES["norm_rows"]):
    m, d = x.shape
    return pl.pallas_call(
        _rmsnorm_kernel,
        grid=(m // tm,),
        in_specs=[pl.BlockSpec((tm, d), lambda i: (i, 0)),
                  pl.BlockSpec((1, d), lambda i: (0, 0))],
        out_specs=pl.BlockSpec((tm, d), lambda i: (i, 0)),
        out_shape=jax.ShapeDtypeStruct((m, d), out_dtype),
        compiler_params=_params(1),
        name="rmsnorm",
    )(x, g.reshape(1, d))


def _mm_kernel(a_ref, w_ref, o_ref):
    o_ref[...] = _dot(a_ref[...], w_ref[...].astype(BF16)).astype(o_ref.dtype)


def _mm_res_kernel(a_ref, w_ref, r_ref, o_ref):
    o_ref[...] = (r_ref[...] + _dot(a_ref[...], w_ref[...].astype(BF16))).astype(o_ref.dtype)


def matmul(a, w, layer, *, tm, tn, n_cols=None, out_dtype=BF16, residual=None, name="matmul"):
    m, k = a.shape
    n = w.shape[2] if n_cols is None else n_cols
    in_specs = [pl.BlockSpec((tm, k), lambda i, j: (i, 0)),
                pl.BlockSpec((None, k, tn), lambda i, j: (layer, 0, j))]
    args = [a, w]
    kernel = _mm_kernel
    if residual is not None:
        in_specs.append(pl.BlockSpec((tm, tn), lambda i, j: (i, j)))
        args.append(residual)
        kernel = _mm_res_kernel
    return pl.pallas_call(
        kernel,
        grid=(m // tm, n // tn),
        in_specs=in_specs,
        out_specs=pl.BlockSpec((tm, tn), lambda i, j: (i, j)),
        out_shape=jax.ShapeDtypeStruct((m, n), out_dtype),
        compiler_params=_params(2),
        name=name,
    )(*args)


NA_Q_ROWS = 4
NA_KEY_ROWS = NA_KH + NA_Q_ROWS


def _na_key_row_start(step, rows):
    return np.clip(step * NA_Q_ROWS - NA_KH // 2, 0, rows - NA_KEY_ROWS)


def _na_bias_table(rpb, rows):
    col = np.arange(GRID_W)
    col_start = np.clip(col - NA_KW // 2, 0, GRID_W - NA_KW)
    rel_col = col[None, :] - col[:, None] + (NA_KW - 1)
    col_ok = (col[None, :] >= col_start[:, None]) & (col[None, :] < col_start[:, None] + NA_KW)
    pick = (np.clip(rel_col, 0, 2 * NA_KW - 2)[None] == np.arange(2 * NA_KW - 1)[:, None, None]).astype(np.float32)
    slab = jnp.einsum("hdr,rcw->hdcw", rpb.astype(F32), pick, precision=lax.Precision.HIGHEST)
    n_steps = rows // NA_Q_ROWS
    rel_row = np.zeros((n_steps, NA_Q_ROWS, NA_KEY_ROWS), np.int64)
    row_ok = np.zeros((n_steps, NA_Q_ROWS, NA_KEY_ROWS), bool)
    for step in range(n_steps):
        for i in range(NA_Q_ROWS):
            r = step * NA_Q_ROWS + i
            win = np.clip(r - NA_KH // 2, 0, rows - NA_KH)
            key_row = _na_key_row_start(step, rows) + np.arange(NA_KEY_ROWS)
            row_ok[step, i] = (key_row >= win) & (key_row < win + NA_KH)
            rel_row[step, i] = key_row - r + (NA_KH - 1)
    assert all((rel_row[s_] * row_ok[s_] == rel_row[1] * row_ok[1]).all() and (row_ok[s_] == row_ok[1]).all()
               for s_ in range(1, n_steps - 1))
    slab = jnp.where(col_ok, slab, MASK_VALUE)
    hidden = jnp.full(slab.shape[:1] + slab.shape[2:], MASK_VALUE, F32)
    cases = []
    for step in (0, 1, n_steps - 1):
        q_rows = [jnp.concatenate([slab[:, rel_row[step, i, k]] if row_ok[step, i, k] else hidden
                                   for k in range(NA_KEY_ROWS)], axis=-1) for i in range(NA_Q_ROWS)]
        cases.append(jnp.concatenate(q_rows, axis=-2))
    return jnp.stack(cases)


def _na_kernel(q_ref, k_ref, v_ref, b_ref, o_ref, *, rows):
    step = pl.program_id(1)
    key_row0 = jnp.clip(step * NA_Q_ROWS - NA_KH // 2, 0, rows - NA_KEY_ROWS)
    k0 = pl.multiple_of(key_row0 * GRID_W, GRID_W)
    n_keys = NA_KEY_ROWS * GRID_W
    scale = HEAD_DIM ** -0.5
    for h in range(NA_HEADS):
        cols = slice(h * HEAD_DIM, (h + 1) * HEAD_DIM)
        k = k_ref[0, pl.ds(k0, n_keys), cols]
        v = v_ref[0, pl.ds(k0, n_keys), cols]
        s = _dot_nt(q_ref[0, :, cols], k) * scale + b_ref[0, h]
        o_ref[0, :, cols] = _softmax_pv(s, v)[0].astype(o_ref.dtype)


def neighbourhood_attention(z, rpb):
    b, s, _ = z.shape
    rows = s // GRID_W
    n_steps = rows // NA_Q_ROWS
    bias = _na_bias_table(rpb, rows)
    tq = NA_Q_ROWS * GRID_W
    bias_case = lambda i, r: (jnp.where(r == 0, 0, jnp.where(r == n_steps - 1, 2, 1)), 0, 0, 0)
    return pl.pallas_call(
        functools.partial(_na_kernel, rows=rows),
        grid=(b, n_steps),
        in_specs=[pl.BlockSpec((1, tq, NA_W), lambda i, r: (i, r, 0)),
                  pl.BlockSpec((1, s, NA_W), lambda i, r: (i, 0, 1)),
                  pl.BlockSpec((1, s, NA_W), lambda i, r: (i, 0, 2)),
                  pl.BlockSpec((1,) + bias.shape[1:], bias_case)],
        out_specs=pl.BlockSpec((1, tq, NA_W), lambda i, r: (i, r, 0)),
        out_shape=jax.ShapeDtypeStruct((b, s, NA_W), BF16),
        compiler_params=_params(2),
        name="neighbourhood_attention",
    )(z, z, z, bias)


def _dft_tables(s):
    half = s // 2
    n = np.arange(half, dtype=np.int64)
    ang = 2.0 * np.pi * ((n[:, None] * n[None, :]) % half) / half
    pos = np.concatenate([np.cos(ang), -np.sin(ang)], axis=1)
    c = np.arange(FN_GROUP_W, dtype=np.int64)
    ang_c = 2.0 * np.pi * ((c[:, None] * c[None, :]) % FN_GROUP_W) / FN_GROUP_W
    chan = np.concatenate([np.cos(ang_c), np.sin(ang_c)], axis=1)
    tw = np.pi * n[:, None] / half * np.ones((1, FN_GROUP_W))
    return (jnp.asarray(pos, F32).astype(BF16), jnp.asarray(chan, F32).astype(BF16),
            jnp.asarray(np.cos(tw), F32), jnp.asarray(np.sin(tw), F32))


def _fourier_kernel(u_ref, chan_ref, twc_ref, tws_ref, pos_ref, o_ref, ve_scr, vo_scr, y_scr, *, s, norm):
    half = s // 2

    @pl.when(pl.program_id(1) == 0)
    def _():
        twc, tws = twc_ref[...], tws_ref[...]
        for g in range(FN_GROUPS):
            cols = slice(g * FN_GROUP_W, (g + 1) * FN_GROUP_W)
            t = _dot(u_ref[0, :, cols], chan_ref[...])
            top, bot = t[:half], t[half:]
            fold = top + bot
            diff = top - bot
            dc, ds = diff[:, :FN_GROUP_W], diff[:, FN_GROUP_W:]
            ve_scr[0:half, cols] = fold[:, :FN_GROUP_W].astype(BF16)
            ve_scr[half:s, cols] = fold[:, FN_GROUP_W:].astype(BF16)
            vo_scr[0:half, cols] = (twc * dc - tws * ds).astype(BF16)
            vo_scr[half:s, cols] = (tws * dc + twc * ds).astype(BF16)

    rows = pos_ref.shape[0]
    y_even = _dot(pos_ref[...], ve_scr[...]) * norm
    y_odd = _dot(pos_ref[...], vo_scr[...]) * norm
    for g in range(FN_GROUPS):
        cols = slice(g * FN_GROUP_W, (g + 1) * FN_GROUP_W)
        y_scr[g, pl.ds(0, rows, stride=2), :] = y_even[:, cols]
        y_scr[g, pl.ds(1, rows, stride=2), :] = y_odd[:, cols]
        o_ref[0, :, cols] = y_scr[g].astype(o_ref.dtype)


def fourier_mix(z, ts=TILES["fourier_rows"]):
    b, s, _ = z.shape
    pos_tab, chan_tab, tw_cos, tw_sin = _dft_tables(s)
    norm = float((s * FN_GROUP_W) ** -0.5)
    whole = lambda a: pl.BlockSpec(a.shape, lambda i, j: (0, 0))
    return pl.pallas_call(
        functools.partial(_fourier_kernel, s=s, norm=norm),
        grid=(b, s // (2 * ts)),
        in_specs=[pl.BlockSpec((1, s, FN_W), lambda i, j: (i, 0, 6)),
                  whole(chan_tab), whole(tw_cos), whole(tw_sin),
                  pl.BlockSpec((ts, s), lambda i, j: (j, 0))],
        out_specs=pl.BlockSpec((1, 2 * ts, FN_W), lambda i, j: (i, j, 0)),
        out_shape=jax.ShapeDtypeStruct((b, s, FN_W), BF16),
        scratch_shapes=[pltpu.VMEM((s, FN_W), BF16), pltpu.VMEM((s, FN_W), BF16),
                        pltpu.VMEM((FN_GROUPS, 2 * ts, FN_GROUP_W), F32)],
        compiler_params=_params(2),
        name="fourier_mix",
    )(z, chan_tab, tw_cos, tw_sin, pos_tab)


def _rope_tables(s):
    half = ROT_DIM // 2
    inv_freq = ROPE_THETA ** (-2.0 * jnp.arange(half, dtype=F32) / ROT_DIM)
    ang = jnp.arange(s, dtype=jnp.int32).astype(F32)[:, None] * inv_freq[None, :]
    cos, sin = jnp.cos(ang), jnp.sin(ang)
    ones = jnp.ones((s, HEAD_DIM - ROT_DIM), F32)
    zeros_half = jnp.zeros((s, half), F32)
    zeros_rest = jnp.zeros((s, HEAD_DIM - ROT_DIM), F32)
    cos_t = jnp.concatenate([cos, cos, ones], axis=1)
    sin_lo = jnp.concatenate([-sin, zeros_half, zeros_rest], axis=1)
    sin_hi = jnp.concatenate([zeros_half, sin, zeros_rest], axis=1)
    return cos_t, sin_lo, sin_hi


def _dilated_kernel(q0, q1, q2, k0, k1, k2, v0, v1, v2, cos_ref, slo_ref, shi_ref, y_ref,
                    rq_ref, rk_ref, rv_ref, o_scr, l_scr, *, seq):
    half = ROT_DIM // 2
    scale = HEAD_DIM ** -0.5
    chunk = 4 * Q_BLOCK
    for g, (q_ref, k_ref, v_ref) in enumerate(((q0, k0, v0), (q1, k1, v1), (q2, k2, v2))):
        for c0 in range(0, seq, chunk):
            rows = slice(c0, c0 + chunk)
            cs, lo, hi = cos_ref[rows, :], slo_ref[rows, :], shi_ref[rows, :]
            for src, dst in ((q_ref, rq_ref), (k_ref, rk_ref)):
                x = src[0, rows, :].astype(F32)
                dst[g, rows, :] = x * cs + pltpu.roll(x, HEAD_DIM - half, 1) * lo + pltpu.roll(x, half, 1) * hi
            rv_ref[g, rows, :] = v_ref[0, rows, :].astype(F32)

    masks = {}

    def window_mask(n_keys, offset):
        if (n_keys, offset) not in masks:
            rel = (lax.broadcasted_iota(jnp.int32, (Q_BLOCK, n_keys), 1)
                   - lax.broadcasted_iota(jnp.int32, (Q_BLOCK, n_keys), 0) + offset)
            masks[(n_keys, offset)] = jnp.abs(rel) <= DIL_HALF_WINDOW
        return masks[(n_keys, offset)]

    for g, d in enumerate(DIL_DILATIONS):
        length = seq // d
        n_keys = min(length, Q_BLOCK + 2 * DIL_HALF_WINDOW)
        for rho in range(d):
            for m_q in range(0, length, Q_BLOCK):
                m_k = min(max(m_q - DIL_HALF_WINDOW, 0), length - n_keys)
                q_rows = pl.ds(rho + d * m_q, Q_BLOCK, stride=d) if d > 1 else pl.ds(m_q, Q_BLOCK)
                k_rows = pl.ds(rho + d * m_k, n_keys, stride=d) if d > 1 else pl.ds(m_k, n_keys)
                q = rq_ref[g, q_rows, :].astype(BF16)
                k = rk_ref[g, k_rows, :].astype(BF16)
                v = rv_ref[g, k_rows, :].astype(BF16)
                s = jnp.where(window_mask(n_keys, m_k - m_q), _dot_nt(q, k) * scale, MASK_VALUE)
                o_scr[g, q_rows, :], l_scr[g, q_rows, :] = _softmax_pv(s, v)

    n_groups = len(DIL_DILATIONS)
    for c0 in range(0, seq, chunk):
        rows = slice(c0, c0 + chunk)
        lses = [l_scr[g, rows, :] for g in range(n_groups)]
        top = functools.reduce(jnp.maximum, lses)
        ws = [jnp.exp(l - top) for l in lses]
        num = sum(w * o_scr[g, rows, :] for g, w in enumerate(ws))
        y_ref[0, rows, :] = (num / sum(ws)).astype(y_ref.dtype)


def dilated_attention(z):
    b, s, _ = z.shape
    n_groups = len(DIL_DILATIONS)
    tables = _rope_tables(s)
    heads_per_block = NA_W // HEAD_DIM

    def head_spec(t, g):
        first = (3 + t) * heads_per_block + g * DIL_HEADS_PER_GROUP
        return pl.BlockSpec((1, s, HEAD_DIM), lambda i, h: (i, 0, first + h))

    table_spec = pl.BlockSpec((s, HEAD_DIM), lambda i, h: (0, 0))
    scratch = pltpu.VMEM((n_groups, s, HEAD_DIM), F32)
    return pl.pallas_call(
        functools.partial(_dilated_kernel, seq=s),
        grid=(b, DIL_HEADS_PER_GROUP),
        in_specs=[head_spec(t, g) for t in range(3) for g in range(n_groups)] + [table_spec] * 3,
        out_specs=pl.BlockSpec((1, s, HEAD_DIM), lambda i, h: (i, 0, h)),
        out_shape=jax.ShapeDtypeStruct((b, s, DIL_GROUP_W), BF16),
        scratch_shapes=[scratch] * 5,
        compiler_params=_params(2),
        name="dilated_attention",
    )(*([z] * 9), *tables).reshape(b * s, DIL_GROUP_W)


def _mem_attn_kernel(q_ref, k_ref, v_ref, o_ref):
    scale = MEM_HEAD_DIM ** -0.5
    col = lax.broadcasted_iota(jnp.int32, (1, MEM_WINDOW), 1)
    parts = []
    for h in range(MEM_HEADS):
        start = h * MEM_HEAD_DIM // LANES * LANES
        win = slice(start, start + MEM_WINDOW)
        in_head = (col >= h * MEM_HEAD_DIM - start) & (col < (h + 1) * MEM_HEAD_DIM - start)
        q = q_ref[0, :, win]
        s = _dot_nt(jnp.where(in_head, q, jnp.zeros_like(q)), k_ref[0, :, win]) * scale
        o = jnp.where(in_head, _softmax_pv(s, v_ref[0, :, win])[0], 0.0)
        pads = (start, MEM_W - start - MEM_WINDOW)
        left, right = (jnp.zeros((o.shape[0], n), F32) for n in pads)
        parts.append(jnp.concatenate([a for a in (left, o, right) if a.shape[1]], axis=1))
    o_ref[0] = sum(parts).astype(o_ref.dtype)


def memory_attention(z, kv, tq=TILES["memory_rows"]):
    b, s, _ = z.shape
    m = kv.shape[1]
    return pl.pallas_call(
        _mem_attn_kernel,
        grid=(b, s // tq),
        in_specs=[pl.BlockSpec((1, tq, MEM_W), lambda i, j: (i, j, 7)),
                  pl.BlockSpec((1, m, MEM_W), lambda i, j: (i, 0, 0)),
                  pl.BlockSpec((1, m, MEM_W), lambda i, j: (i, 0, 1))],
        out_specs=pl.BlockSpec((1, tq, MEM_W), lambda i, j: (i, j, 0)),
        out_shape=jax.ShapeDtypeStruct((b, s, MEM_W), BF16),
        compiler_params=_params(2),
        name="memory_attention",
    )(z, kv, kv)


def _merge_kernel(h_ref, ya_ref, yb_ref, yc_ref, ym_ref, wg0, wg1, wg2, wg3, bg0, bg1, bg2, bg3,
                  wa_ref, wb_ref, wc_ref, wm_ref, o_ref):
    h = h_ref[...]
    acc = None
    for y_ref, wg, bg, wbr in ((ya_ref, wg0, bg0, wa_ref), (yb_ref, wg1, bg1, wb_ref),
                               (yc_ref, wg2, bg2, wc_ref), (ym_ref, wg3, bg3, wm_ref)):
        gate = jax.nn.sigmoid(_dot(h, wg[...].astype(BF16)) + bg[...])
        term = gate * _dot(y_ref[...], wbr[...].astype(BF16))
        acc = term if acc is None else acc + term
    o_ref[...] = acc.astype(o_ref.dtype)


def _gate_block(i, j, *, layer, branch, n_tiles):
    return (layer, 0, branch * n_tiles + j)


def gated_merge(h, ys, w_gate, b_gate, w_brs, layers, tm=TILES["merge"][0], tn=TILES["merge"][1]):
    m, d = h.shape
    n_tiles = d // tn
    row = lambda width: pl.BlockSpec((tm, width), lambda i, j: (i, 0))
    gate_maps = [functools.partial(_gate_block, layer=layers[0], branch=br, n_tiles=n_tiles) for br in range(4)]
    gate_w = [pl.BlockSpec((None, d, tn), gm) for gm in gate_maps]
    gate_b = [pl.BlockSpec((None, 1, tn), gm) for gm in gate_maps]
    br_w = [pl.BlockSpec((None, w.shape[1], tn), functools.partial(lambda i, j, l: (l, 0, j), l=l))
            for w, l in zip(w_brs, layers[1:])]
    b_gate3 = b_gate.reshape(b_gate.shape[0], 1, -1)
    return pl.pallas_call(
        _merge_kernel,
        grid=(m // tm, n_tiles),
        in_specs=[row(d)] + [row(y.shape[1]) for y in ys] + gate_w + gate_b + br_w,
        out_specs=pl.BlockSpec((tm, tn), lambda i, j: (i, j)),
        out_shape=jax.ShapeDtypeStruct((m, d), BF16),
        compiler_params=_params(2),
        name="gated_merge",
    )(h, *ys, w_gate, w_gate, w_gate, w_gate, b_gate3, b_gate3, b_gate3, b_gate3, *w_brs)


def _silu_mul(g, u):
    return g * jax.nn.sigmoid(g) * u


def _swiglu_up_kernel(x_ref, g_ref, wg_ref, wu_ref, o_ref, h_scr):
    @pl.when(pl.program_id(1) == 0)
    def _():
        h_scr[...] = _rmsnorm_body(x_ref[...], g_ref[...]).astype(h_scr.dtype)

    a = h_scr[...]
    o_ref[...] = _silu_mul(_dot(a, wg_ref[...].astype(BF16)), _dot(a, wu_ref[...].astype(BF16))).astype(o_ref.dtype)


def swiglu_up(x, norm_g, wg, wu, layer, tm=TILES["swiglu_up"][0], tf=TILES["swiglu_up"][1]):
    m, k = x.shape
    f = wg.shape[2]
    return pl.pallas_call(
        _swiglu_up_kernel,
        grid=(m // tm, f // tf),
        in_specs=[pl.BlockSpec((tm, k), lambda i, j: (i, 0)),
                  pl.BlockSpec((1, k), lambda i, j: (0, 0)),
                  pl.BlockSpec((None, k, tf), lambda i, j: (layer, 0, j)),
                  pl.BlockSpec((None, k, tf), lambda i, j: (layer, 0, j))],
        out_specs=pl.BlockSpec((tm, tf), lambda i, j: (i, j)),
        out_shape=jax.ShapeDtypeStruct((m, f), BF16),
        scratch_shapes=[pltpu.VMEM((tm, k), BF16)],
        compiler_params=_params(2),
        name="swiglu_up",
    )(x, norm_g.reshape(1, k), wg, wu)


def _router_kernel(x_ref, g_ref, r_ref, idx_ref, rank_ref, gate_ref, cnt_ref):
    @pl.when(pl.program_id(0) == 0)
    def _():
        cnt_ref[...] = jnp.zeros_like(cnt_ref)

    h = _rmsnorm_body(x_ref[...], g_ref[...])
    r = r_ref[...]
    h_hi, r_hi = h.astype(BF16), r.astype(BF16)
    h_lo, r_lo = (h - h_hi.astype(F32)).astype(BF16), (r - r_hi.astype(F32)).astype(BF16)
    logits = _dot(h_hi, r_hi) + (_dot(h_hi, r_lo) + _dot(h_lo, r_hi))
    lane = lax.broadcasted_iota(jnp.int32, logits.shape, 1).astype(F32)
    neg = -jnp.inf
    lg = jnp.where(lane < N_EXPERTS, logits, neg)
    v1 = jnp.max(lg, axis=-1, keepdims=True)
    i1 = jnp.min(jnp.where(lg == v1, lane, float(LANES)), axis=-1, keepdims=True)
    lg2 = jnp.where(lane == i1, neg, lg)
    v2 = jnp.max(lg2, axis=-1, keepdims=True)
    i2 = jnp.min(jnp.where(lg2 == v2, lane, float(LANES)), axis=-1, keepdims=True)
    e = jnp.exp(v2 - v1)
    g1 = 1.0 / (1.0 + e)
    g2 = e / (1.0 + e)
    onehot = jnp.where((lane == i1) | (lane == i2), 1.0, 0.0)
    tt = onehot.shape[0]
    earlier = lax.broadcasted_iota(jnp.int32, (tt, tt), 1) < lax.broadcasted_iota(jnp.int32, (tt, tt), 0)
    pos = _dot(jnp.where(earlier, 1.0, 0.0).astype(BF16), onehot.astype(BF16)) + cnt_ref[...]
    r1 = jnp.sum(jnp.where(lane == i1, pos, 0.0), axis=-1, keepdims=True)
    r2 = jnp.sum(jnp.where(lane == i2, pos, 0.0), axis=-1, keepdims=True)
    cnt_ref[...] += jnp.sum(onehot, axis=0, keepdims=True)
    idx_ref[...] = jnp.where(lane == 0, i1, jnp.where(lane == 1, i2, 0.0)).astype(jnp.int32)
    rank_ref[...] = jnp.where(lane == 0, r1, jnp.where(lane == 1, r2, 0.0)).astype(jnp.int32)
    gate_ref[...] = jnp.where(lane == 0, g1, jnp.where(lane == 1, g2, 0.0))


def route(x, g, router, tm=TILES["route_rows"]):
    m, d = x.shape
    r_pad = jnp.pad(router, ((0, 0), (0, LANES - N_EXPERTS)))
    lane_spec = pl.BlockSpec((tm, LANES), lambda i: (i, 0))
    return pl.pallas_call(
        _router_kernel,
        grid=(m // tm,),
        in_specs=[pl.BlockSpec((tm, d), lambda i: (i, 0)),
                  pl.BlockSpec((1, d), lambda i: (0, 0)),
                  pl.BlockSpec((d, LANES), lambda i: (0, 0))],
        out_specs=[lane_spec, lane_spec, lane_spec, pl.BlockSpec((1, LANES), lambda i: (0, 0))],
        out_shape=[jax.ShapeDtypeStruct((m, LANES), jnp.int32),
                   jax.ShapeDtypeStruct((m, LANES), jnp.int32),
                   jax.ShapeDtypeStruct((m, LANES), F32),
                   jax.ShapeDtypeStruct((1, LANES), F32)],
        compiler_params=_params(1),
        name="route",
    )(x, g.reshape(1, d), r_pad)


def _dispatch_kernel(dest_ref, lo_ref, hi_ref, na_ref, x_hbm, g_ref, a_ref, src_ref, rows_ref, sems):
    j = pl.program_id(0)
    n_groups8, d = rows_ref.shape[1], rows_ref.shape[3]
    tm = n_groups8 * SUBLANES
    n_active = na_ref[0]

    def row_copy(tile, i, u):
        slot = lax.rem(tile, 2)
        return pltpu.make_async_copy(x_hbm.at[pl.ds(src_ref[tile * tm + SUBLANES * i + u], 1)],
                                     rows_ref.at[slot, i, pl.ds(u, 1)], sems.at[slot])

    def start_tile(tile):
        def start(i, c):
            for u in range(SUBLANES):
                row_copy(tile, i, u).start(priority=u % N_DMA_PRIORITIES)
            return c

        lax.fori_loop(0, n_groups8, start, 0)

    @pl.when(j == 0)
    def _():
        def clear(i, c):
            src_ref[i] = 0
            return c

        for e in range(N_EXPERTS):
            lax.fori_loop(lo_ref[e], hi_ref[e], clear, 0)

        def fill(i, c):
            src_ref[dest_ref[i]] = lax.shift_right_logical(i, 1)
            return c

        lax.fori_loop(0, dest_ref.shape[0], fill, 0, unroll=8)

        @pl.when(n_active > 0)
        def _():
            start_tile(0)

    @pl.when(j + 1 < n_active)
    def _():
        start_tile(j + 1)

    @pl.when(j < n_active)
    def _():
        def wait(i, c):
            for u in range(SUBLANES):
                row_copy(j, i, u).wait()
            return c

        lax.fori_loop(0, n_groups8, wait, 0)
        rows = rows_ref[lax.rem(j, 2)].reshape(tm, d)
        a_ref[...] = _rmsnorm_body(rows, g_ref[...]).astype(a_ref.dtype)

    @pl.when(j >= n_active)
    def _():
        a_ref[...] = jnp.zeros_like(a_ref)


def _combine_kernel(dest_ref, x_ref, gate_ref, g_ref, y_hbm, o_ref, rows_ref, sems, *, apply_norm):
    tt, d = x_ref.shape
    i = pl.program_id(0)

    def row_copy(tile, r8, u, k):
        slot = lax.rem(tile, 2)
        row = dest_ref[TOP_K * (tile * tt + SUBLANES * r8 + u) + k]
        return pltpu.make_async_copy(y_hbm.at[pl.ds(row, 1)], rows_ref.at[slot, k, r8, pl.ds(u, 1)], sems.at[slot])

    def start_tile(tile):
        def start(r8, c):
            for u in range(SUBLANES):
                for k in range(TOP_K):
                    row_copy(tile, r8, u, k).start(priority=k % N_DMA_PRIORITIES)
            return c

        lax.fori_loop(0, tt // SUBLANES, start, 0)

    @pl.when(i == 0)
    def _():
        start_tile(0)

    @pl.when(i + 1 < pl.num_programs(0))
    def _():
        start_tile(i + 1)

    def wait(r8, c):
        for u in range(SUBLANES):
            for k in range(TOP_K):
                row_copy(i, r8, u, k).wait()
        return c

    lax.fori_loop(0, tt // SUBLANES, wait, 0)
    gates = gate_ref[...]
    slot = lax.rem(i, 2)
    y = x_ref[...]
    for k in range(TOP_K):
        y = y + gates[:, k:k + 1] * rows_ref[slot, k].reshape(tt, d)
    o_ref[...] = _rmsnorm_body(y, g_ref[...]) if apply_norm else y


def _on_valid_rows(n_valid, o_ref, compute):
    tm = o_ref.shape[0]
    for n in range(MOE_ROW_QUANTUM, tm + 1, MOE_ROW_QUANTUM):
        @pl.when((n_valid > n - MOE_ROW_QUANTUM) & (n_valid <= n))
        def _(n=n):
            o_ref[:n, :] = compute(n).astype(o_ref.dtype)
            if n < tm:
                o_ref[n:, :] = jnp.zeros((tm - n, o_ref.shape[1]), o_ref.dtype)

    @pl.when(n_valid == 0)
    def _():
        o_ref[...] = jnp.zeros_like(o_ref)


def _stream_expert_weights(meta, w_hbms, w_scr, sems, layer, width, w_rows=slice(None)):
    te_ref, first_ref, group_ref, next_ref, n_groups_ref = meta
    c, j = pl.program_id(0), pl.program_id(1)
    group = group_ref[j]
    n_groups = n_groups_ref[0]
    block = c * n_groups + group
    slot = lax.rem(block, 2)

    def copies(expert, col, to_slot):
        cols = pl.ds(pl.multiple_of(col * width, width), width)
        return [pltpu.make_async_copy(w.at[layer, expert, w_rows, cols], w_scr.at[to_slot, i], sems.at[to_slot, i])
                for i, w in enumerate(w_hbms)]

    @pl.when(first_ref[j] == 1)
    def _():
        @pl.when(block == 0)
        def _():
            for cp in copies(te_ref[j], c, slot):
                cp.start()

        for cp in copies(te_ref[j], c, slot):
            cp.wait()
        more_groups = group + 1 < n_groups

        @pl.when(more_groups)
        def _():
            for cp in copies(next_ref[j], c, 1 - slot):
                cp.start()

        @pl.when(jnp.logical_not(more_groups) & (c + 1 < pl.num_programs(0)))
        def _():
            for cp in copies(te_ref[0], c + 1, 1 - slot):
                cp.start()

    return slot


def _moe_up_kernel(te_ref, na_ref, nv_ref, first_ref, group_ref, next_ref, ng_ref, a_ref, wg_hbm, wu_hbm, o_ref,
                   w_scr, sems, *, layer):
    meta = (te_ref, first_ref, group_ref, next_ref, ng_ref)
    slot = _stream_expert_weights(meta, (wg_hbm, wu_hbm), w_scr, sems, layer, o_ref.shape[1])

    def compute(n):
        a = a_ref[:n, :]
        return _silu_mul(_dot(a, w_scr[slot, 0].astype(BF16)), _dot(a, w_scr[slot, 1].astype(BF16)))

    _on_valid_rows(nv_ref[pl.program_id(1)], o_ref, compute)


def _moe_down_kernel(te_ref, na_ref, nv_ref, first_ref, group_ref, next_ref, ng_ref, a_ref, w_hbm, *rest,
                     layer, tk, k0):
    partial_ref = rest[0] if len(rest) == 4 else None
    o_ref, w_scr, sems = rest[-3:]
    meta = (te_ref, first_ref, group_ref, next_ref, ng_ref)
    slot = _stream_expert_weights(meta, (w_hbm,), w_scr, sems, layer, o_ref.shape[1],
                                  pl.ds(k0, a_ref.shape[1]))

    def compute(n):
        acc = None if partial_ref is None else partial_ref[:n, :]
        for c in range(a_ref.shape[1] // tk):
            part = _dot(a_ref[:n, c * tk:(c + 1) * tk], w_scr[slot, 0, c * tk:(c + 1) * tk, :].astype(BF16))
            acc = part if acc is None else acc + part
        return acc

    _on_valid_rows(nv_ref[pl.program_id(1)], o_ref, compute)


def moe_layer(x, norm_g, router, wg, wu, wd, layer, out_g, tm=TILES["moe_rows"], tf=TILES["moe_up_cols"],
              tn=TILES["moe_down_cols"], tk=TILES["moe_down_k"], tt=TILES["combine_rows"]):
    assert TOP_K == 2
    t, d = x.shape
    _, n_exp, _, f = wg.shape
    n_rows = t * TOP_K
    n_tiles = n_rows // tm + n_exp
    idx, rank, gate, cnt = route(x, norm_g, router)
    counts = cnt[0, :n_exp].astype(jnp.int32)
    tiles_per = (counts + tm - 1) // tm
    tile_end = jnp.cumsum(tiles_per)
    start = (tile_end - tiles_per) * tm
    n_active = tile_end[-1:]
    tile_ids = jnp.arange(n_tiles, dtype=jnp.int32)
    tile_expert = jnp.sum((tile_ids[:, None] >= tile_end[None, :]).astype(jnp.int32), axis=1)
    last_expert = jnp.sum((n_active - 1 >= tile_end).astype(jnp.int32))
    tile_expert = jnp.where(tile_ids < n_active, tile_expert, last_expert).astype(jnp.int32)
    group_end = start + counts
    tile_valid = jnp.where(tile_ids < n_active, jnp.clip(group_end[tile_expert] - tile_ids * tm, 0, tm), 0)
    prev_expert = jnp.concatenate([jnp.full((1,), -1, jnp.int32), tile_expert[:-1]])
    tile_first = ((tile_ids < n_active) & (tile_expert != prev_expert)).astype(jnp.int32)
    tile_group = jnp.cumsum(tile_first) - 1
    n_groups = jnp.sum(tile_first, keepdims=True)
    group_expert = jnp.argsort(tiles_per == 0, stable=True).astype(jnp.int32)
    tile_next = group_expert[jnp.minimum(tile_group + 1, n_exp - 1)]
    tile_meta = (tile_expert, n_active, tile_valid, tile_first, tile_group, tile_next, n_groups)
    picked = idx[:, :TOP_K, None] == jnp.arange(n_exp, dtype=jnp.int32)
    dest = (jnp.sum(jnp.where(picked, start, 0), axis=-1) + rank[:, :TOP_K]).reshape(n_rows)

    row_tile = lambda j, na: jnp.maximum(jnp.minimum(j, na[0] - 1), 0)
    a = pl.pallas_call(
        _dispatch_kernel,
        grid_spec=pltpu.PrefetchScalarGridSpec(
            num_scalar_prefetch=4,
            grid=(n_tiles,),
            in_specs=[pl.BlockSpec(memory_space=pl.ANY),
                      pl.BlockSpec((1, d), lambda j, dst, lo, hi, na: (0, 0))],
            out_specs=pl.BlockSpec((tm, d), lambda j, dst, lo, hi, na: (j, 0)),
            scratch_shapes=[pltpu.SMEM((n_tiles * tm,), jnp.int32),
                            pltpu.VMEM((2, tm // SUBLANES, SUBLANES, d), F32),
                            pltpu.SemaphoreType.DMA((2,))]),
        out_shape=jax.ShapeDtypeStruct((n_tiles * tm, d), BF16),
        compiler_params=_params(1),
        name="moe_dispatch",
    )(dest, group_end, tile_end * tm, n_active, x, norm_g.reshape(1, d))
    lhs_tile = lambda c, j, te, na, *_: (row_tile(j, na), 0)
    out_tile = lambda c, j, *_: (j, c)
    in_hbm = pl.BlockSpec(memory_space=pl.ANY)
    up = pl.pallas_call(
        functools.partial(_moe_up_kernel, layer=layer),
        grid_spec=pltpu.PrefetchScalarGridSpec(
            num_scalar_prefetch=len(tile_meta),
            grid=(f // tf, n_tiles),
            in_specs=[pl.BlockSpec((tm, d), lhs_tile), in_hbm, in_hbm],
            out_specs=pl.BlockSpec((tm, tf), out_tile),
            scratch_shapes=[pltpu.VMEM((2, 2, d, tf), F32), pltpu.SemaphoreType.DMA((2, 2))]),
        out_shape=jax.ShapeDtypeStruct((n_tiles * tm, f), BF16),
        compiler_params=_params(2),
        name="moe_up",
    )(*tile_meta, a, wg, wu)
    fk = f // MOE_DOWN_K_SPLITS
    down = None
    for part in range(MOE_DOWN_K_SPLITS):
        lhs_part = functools.partial(lambda c, j, te, na, *_, part: (row_tile(j, na), part), part=part)
        extra_specs = [] if down is None else [pl.BlockSpec((tm, tn), out_tile)]
        extra_args = [] if down is None else [down]
        down = pl.pallas_call(
            functools.partial(_moe_down_kernel, layer=layer, tk=tk, k0=part * fk),
            grid_spec=pltpu.PrefetchScalarGridSpec(
                num_scalar_prefetch=len(tile_meta),
                grid=(d // tn, n_tiles),
                in_specs=[pl.BlockSpec((tm, fk), lhs_part), in_hbm] + extra_specs,
                out_specs=pl.BlockSpec((tm, tn), out_tile),
                scratch_shapes=[pltpu.VMEM((2, 1, fk, tn), F32), pltpu.SemaphoreType.DMA((2, 1))]),
            out_shape=jax.ShapeDtypeStruct((n_tiles * tm, d), F32),
            compiler_params=_params(2),
            name="moe_down",
        )(*tile_meta, up, wd, *extra_args)
    g_out = jnp.ones((d,), F32) if out_g is None else out_g
    return pl.pallas_call(
        functools.partial(_combine_kernel, apply_norm=out_g is not None),
        grid_spec=pltpu.PrefetchScalarGridSpec(
            num_scalar_prefetch=1,
            grid=(t // tt,),
            in_specs=[pl.BlockSpec((tt, d), lambda i, dst: (i, 0)),
                      pl.BlockSpec((tt, LANES), lambda i, dst: (i, 0)),
                      pl.BlockSpec((1, d), lambda i, dst: (0, 0)),
                      pl.BlockSpec(memory_space=pl.ANY)],
            out_specs=pl.BlockSpec((tt, d), lambda i, dst: (i, 0)),
            scratch_shapes=[pltpu.VMEM((2, TOP_K, tt // SUBLANES, SUBLANES, d), F32),
                            pltpu.SemaphoreType.DMA((2,))]),
        out_shape=jax.ShapeDtypeStruct((t, d), F32),
        compiler_params=_params(1),
        name="moe_combine",
    )(dest, x, gate, g_out.reshape(1, d), down)


def _mixer(x, mem_n, b, s, layer, w_in, rpb, w_mem_kv, w_br_a, w_br_b, w_br_c, w_br_m, w_gate, b_gate, w_out,
           norm_g):
    t = b * s
    h = rmsnorm(x, norm_g, BF16)
    assert TILES["in_proj"][1] == NA_W == FN_W == MEM_W
    z = matmul(h, w_in, layer, tm=TILES["in_proj"][0], tn=TILES["in_proj"][1], name="in_proj").reshape(b, s, -1)
    kv_m = matmul(mem_n, w_mem_kv, layer, tm=mem_n.shape[0], tn=TILES["kv_proj_cols"], name="mem_kv_proj")
    y_a = neighbourhood_attention(z, rpb).reshape(t, NA_W)
    y_b = fourier_mix(z).reshape(t, FN_W)
    y_c = dilated_attention(z)
    y_m = memory_attention(z, kv_m.reshape(b, -1, 2 * MEM_W)).reshape(t, MEM_W)
    merged = gated_merge(h, (y_a, y_b, y_c, y_m), w_gate, b_gate, (w_br_a, w_br_b, w_br_c, w_br_m),
                         (layer,) * 5)
    return matmul(merged, w_out, layer, tm=TILES["out_proj"][0], tn=TILES["out_proj"][1], out_dtype=F32,
                  residual=x, name="out_proj")


def kernel(x, mem, norm_mix, w_in, rpb, norm_mem, w_mem_kv, w_br_a, w_br_b, w_br_c, w_br_m, w_gate, b_gate,
           w_out, norm_ffn, dense_w_gate, dense_w_up, dense_w_down, router, exp_w_gate, exp_w_up, exp_w_down,
           final_norm):
    b, s, d = x.shape
    depth = norm_mix.shape[0]
    t = b * s
    xf = x.reshape(t, d)
    memf = mem.reshape(-1, d)
    normed = False
    for layer in range(depth):
        mem_n = rmsnorm(memf, norm_mem[layer], BF16)
        xf = _mixer(xf, mem_n, b, s, layer, w_in, rpb[layer], w_mem_kv, w_br_a, w_br_b, w_br_c, w_br_m, w_gate,
                    b_gate, w_out, norm_mix[layer])
        i = layer // 2
        if layer % 2 == 0:
            u = swiglu_up(xf, norm_ffn[layer], dense_w_gate, dense_w_up, i)
            xf = matmul(u, dense_w_down, i, tm=TILES["swiglu_down"][0], tn=TILES["swiglu_down"][1],
                        out_dtype=F32, residual=xf, name="swiglu_down")
        else:
            normed = layer == depth - 1
            xf = moe_layer(xf, norm_ffn[layer], router[i], exp_w_gate, exp_w_up, exp_w_down, i,
                           final_norm if normed else None)
    if not normed:
        xf = rmsnorm(xf, final_norm, F32)
    return xf.reshape(b, s, d)
```
